```python
import math
import jax, jax.numpy as jnp
from jax import lax
import numpy as np

D_MODEL = 2048
BATCH = 4
SEQ = 8192
DEPTH = 1

CHUNK = 64
Q_BLOCK = 128
ATT_HEADS = 4
ATT_HEAD_DIM = 128
ATT_V_DIM = 2 * ATT_HEAD_DIM
ATT_WIDTH = ATT_HEADS * ATT_V_DIM
ROPE_THETA = 10000.0
CONV_CH = D_MODEL - ATT_WIDTH
CONV_WIDTH = 31
QK_COLS = ATT_HEADS * 2 * ATT_HEAD_DIM
IN_COLS = 2 * QK_COLS + ATT_WIDTH + 2 * CONV_CH
MIX_WIDTH = ATT_WIDTH + CONV_CH
FFN_HIDDEN = int(math.ceil(8 * D_MODEL / 3 / 256) * 256)
EPS = 1e-6
LN_EPS = 1e-5

kernel_name = "hybrid_diffattn_conformerconv_block"


def rms_norm(x, g, eps=EPS):
    xf = x.astype(jnp.float32)
    y = xf * lax.rsqrt(jnp.mean(xf * xf, axis=-1, keepdims=True) + eps)
    return (y * g.astype(jnp.float32)).astype(x.dtype)


def layer_norm(x, g, b, eps=LN_EPS):
    xf = x.astype(jnp.float32)
    mu = jnp.mean(xf, axis=-1, keepdims=True)
    var = jnp.mean(jnp.square(xf - mu), axis=-1, keepdims=True)
    y = (xf - mu) * lax.rsqrt(var + eps)
    return (y * g.astype(jnp.float32) + b.astype(jnp.float32)).astype(x.dtype)


def rope_tables(seq, dim):
    inv_freq = ROPE_THETA ** (-jnp.arange(0, dim, 2, dtype=jnp.float32) / dim)
    ang = jnp.arange(seq, dtype=jnp.float32)[:, None] * inv_freq[None, :]
    ang = jnp.concatenate([ang, ang], axis=-1)
    return jnp.cos(ang), jnp.sin(ang)


def apply_rope(t, cos, sin):
    half = t.shape[-1] // 2
    t1, t2 = t[..., :half], t[..., half:]
    rot = jnp.concatenate([-t2, t1], axis=-1)
    c = cos[None, :, None, None, :].astype(t.dtype)
    s = sin[None, :, None, None, :].astype(t.dtype)
    return t * c + rot * s


def diff_attention(q, k, v, lam):
    B, S, H, _, d = q.shape
    E = v.shape[-1]
    nb = S // Q_BLOCK
    scale = 1.0 / math.sqrt(d)
    qb = q.reshape(B, nb, Q_BLOCK, H, 2, d).transpose(1, 0, 2, 3, 4, 5)
    key_chunk = jnp.arange(S) // CHUNK

    def one_block(args):
        q_blk, i = args
        q_chunk = (i * Q_BLOCK + jnp.arange(Q_BLOCK)) // CHUNK
        s = jnp.einsum('bqhcd,bkhcd->bhcqk', q_blk, k).astype(jnp.float32) * scale
        mask = key_chunk[None, :] <= q_chunk[:, None]
        s = jnp.where(mask[None, None, None], s, -jnp.inf)
        p = jax.nn.softmax(s, axis=-1)
        w = p[:, :, 0] - lam * p[:, :, 1]
        return jnp.einsum('bhqk,bkhe->bqhe', w.astype(v.dtype), v)

    out = lax.map(one_block, (qb, jnp.arange(nb)))
    return out.transpose(1, 0, 2, 3, 4).reshape(B, S, H, E)


def causal_depthwise_conv(u, w, b):
    y = lax.conv_general_dilated(
        u, w.astype(u.dtype), window_strides=(1,), padding=[(CONV_WIDTH - 1, 0)],
        dimension_numbers=('NWC', 'WIO', 'NWC'), feature_group_count=u.shape[-1])
    return y + b.astype(u.dtype)


def setup_inputs(seed: int = 0) -> dict:
    key = jax.random.key(seed)
    ks = jax.random.split(key, 20)
    f32 = jnp.float32
    L = DEPTH
    d = ATT_HEAD_DIM
    nrm = lambda k, shape, s: jax.random.normal(k, shape, f32) * s
    return {
        "x": jax.random.normal(ks[0], (BATCH, SEQ, D_MODEL), f32),
        "norm1_g": 1.0 + nrm(ks[1], (L, D_MODEL), 0.02),
        "w_in": nrm(ks[2], (L, D_MODEL, IN_COLS), D_MODEL ** -0.5),
        "q_norm_g": 1.0 + nrm(ks[3], (L, d), 0.02),
        "k_norm_g": 1.0 + nrm(ks[4], (L, d), 0.02),
        "lambda_q1": nrm(ks[5], (L, d), 0.1),
        "lambda_k1": nrm(ks[6], (L, d), 0.1),
        "lambda_q2": nrm(ks[7], (L, d), 0.1),
        "lambda_k2": nrm(ks[8], (L, d), 0.1),
        "subln_g": 1.0 + nrm(ks[9], (L, ATT_V_DIM), 0.02),
        "conv_w": nrm(ks[10], (L, CONV_WIDTH, 1, CONV_CH), CONV_WIDTH ** -0.5),
        "conv_b": nrm(ks[11], (L, CONV_CH), 0.02),
        "conv_ln_g": 1.0 + nrm(ks[12], (L, CONV_CH), 0.02),
        "conv_ln_b": nrm(ks[13], (L, CONV_CH), 0.02),
        "w_out": nrm(ks[14], (L, MIX_WIDTH, D_MODEL), MIX_WIDTH ** -0.5),
        "norm2_g": 1.0 + nrm(ks[15], (L, D_MODEL), 0.02),
        "w_gate": nrm(ks[16], (L, D_MODEL, FFN_HIDDEN), D_MODEL ** -0.5),
        "w_up": nrm(ks[17], (L, D_MODEL, FFN_HIDDEN), D_MODEL ** -0.5),
        "w_down": nrm(ks[18], (L, FFN_HIDDEN, D_MODEL), FFN_HIDDEN ** -0.5),
    }


def reference(x, norm1_g, w_in, q_norm_g, k_norm_g, lambda_q1, lambda_k1,
              lambda_q2, lambda_k2, subln_g, conv_w, conv_b, conv_ln_g,
              conv_ln_b, w_out, norm2_g, w_gate, w_up, w_down):
    B, S, D = x.shape
    H, d = ATT_HEADS, ATT_HEAD_DIM
    cos, sin = rope_tables(S, d)
    h = x
    for l in range(DEPTH):
        lambda_init = 0.8 - 0.6 * math.exp(-0.3 * l)
        xn = rms_norm(h, norm1_g[l])
        proj = jnp.einsum('bsd,dn->bsn', xn, w_in[l].astype(xn.dtype))
        q, k, v, ga, gg = jnp.split(
            proj, np.cumsum([QK_COLS, QK_COLS, ATT_WIDTH, CONV_CH]).tolist(), axis=-1)
        q = q.reshape(B, S, H, 2, d)
        k = k.reshape(B, S, H, 2, d)
        v = v.reshape(B, S, H, ATT_V_DIM)
        q = apply_rope(rms_norm(q, q_norm_g[l]), cos, sin)
        k = apply_rope(rms_norm(k, k_norm_g[l]), cos, sin)
        lam = (jnp.exp(jnp.sum(lambda_q1[l].astype(jnp.float32) * lambda_k1[l].astype(jnp.float32)))
               - jnp.exp(jnp.sum(lambda_q2[l].astype(jnp.float32) * lambda_k2[l].astype(jnp.float32)))
               + lambda_init)
        att = diff_attention(q, k, v, lam)
        att = rms_norm(att, subln_g[l]) * (1.0 - lambda_init)
        att = att.reshape(B, S, ATT_WIDTH)
        u = ga * jax.nn.sigmoid(gg)
        u = causal_depthwise_conv(u, conv_w[l], conv_b[l])
        u = jax.nn.silu(layer_norm(u, conv_ln_g[l], conv_ln_b[l]))
        mixed = jnp.concatenate([att, u], axis=-1)
        h = h + jnp.einsum('bsm,md->bsd', mixed, w_out[l].astype(mixed.dtype))
        hn = rms_norm(h, norm2_g[l])
        a = jnp.einsum('bsd,df->bsf', hn, w_gate[l].astype(hn.dtype))
        bu = jnp.einsum('bsd,df->bsf', hn, w_up[l].astype(hn.dtype))
        h = h + jnp.einsum('bsf,fd->bsd', jax.nn.silu(a) * bu, w_down[l].astype(hn.dtype))
    return h
```

```python
import functools
import math

import jax
import jax.numpy as jnp
from jax import lax
from jax.experimental import pallas as pl
from jax.experimental.pallas import tpu as pltpu

ATT_HEADS = 4
HEAD_DIM = 128
V_DIM = 2 * HEAD_DIM
ATT_WIDTH = ATT_HEADS * V_DIM
CHUNK = 64
CONV_WIDTH = 31
CONV_HALO = 32
ROPE_THETA = 10000.0
EPS = 1e-6
LN_EPS = 1e-5

SUBLANES = 8
LANES = 128
VMEM_LIMIT_BYTES = 58 * 1024 * 1024

F32 = jnp.float32
BF16 = jnp.bfloat16


def _params(*semantics):
    return pltpu.CompilerParams(dimension_semantics=semantics, vmem_limit_bytes=VMEM_LIMIT_BYTES)


def _resident(shape):
    return pl.BlockSpec(shape, lambda *_: (0,) * len(shape), pipeline_mode=pl.Buffered(1))


def _rms(t, eps=EPS):
    return t * lax.rsqrt(jnp.mean(t * t, axis=-1, keepdims=True) + eps)


def _in_proj_kernel(x_ref, g1_ref, w_ref, qg_ref, kg_ref, cos_ref, sin_ref,
                    q_ref, k_ref, v_ref, u_ref, *, qk_cols, att_w, conv_ch):
    xn = (_rms(x_ref[...]) * g1_ref[...]).astype(BF16)
    cos = cos_ref[...]
    sin = sin_ref[...]

    def proj(c0, width=V_DIM):
        return jnp.dot(xn, w_ref[:, c0:c0 + width], preferred_element_type=F32)

    def norm_rope(t, g, scale):
        y = _rms(t) * g
        y = y * cos + pltpu.roll(y, HEAD_DIM // 2, 1) * sin
        return (y * scale).astype(BF16)

    for o_ref, g_ref, base, scale in ((q_ref, qg_ref, 0, 1.0 / math.sqrt(HEAD_DIM)),
                                      (k_ref, kg_ref, qk_cols, 1.0)):
        for h in range(qk_cols // V_DIM):
            t = proj(base + h * V_DIM)
            for c in range(2):
                lo = h * V_DIM + c * HEAD_DIM
                o_ref[:, lo:lo + HEAD_DIM] = norm_rope(t[:, c * HEAD_DIM:(c + 1) * HEAD_DIM], g_ref[...], scale)
    for h in range(att_w // V_DIM):
        v_ref[:, h * V_DIM:(h + 1) * V_DIM] = proj(2 * qk_cols + h * V_DIM).astype(BF16)
    for j in range(conv_ch // V_DIM):
        ga = proj(2 * qk_cols + att_w + j * V_DIM)
        gg = proj(2 * qk_cols + att_w + conv_ch + j * V_DIM)
        u_ref[:, j * V_DIM:(j + 1) * V_DIM] = (ga * jax.nn.sigmoid(gg)).astype(BF16)


def _in_proj(x2, g1, w_in, qg, kg, cos, sin, *, seq, tm):
    T, D = x2.shape
    qk_cols = ATT_HEADS * 2 * HEAD_DIM
    conv_ch = (w_in.shape[1] - 2 * qk_cols - ATT_WIDTH) // 2
    n_pos = seq // tm
    row = lambda w: pl.BlockSpec((tm, w), lambda i: (i, 0))
    pos = pl.BlockSpec((tm, HEAD_DIM), lambda i: (i % n_pos, 0))
    return pl.pallas_call(
        functools.partial(_in_proj_kernel, qk_cols=qk_cols, att_w=ATT_WIDTH, conv_ch=conv_ch),
        grid=(T // tm,),
        in_specs=[row(D), _resident((1, D)), _resident(w_in.shape), _resident((1, HEAD_DIM)),
                  _resident((1, HEAD_DIM)), pos, pos],
        out_specs=[row(qk_cols), row(qk_cols), row(ATT_WIDTH), row(conv_ch)],
        out_shape=[jax.ShapeDtypeStruct((T, qk_cols), BF16), jax.ShapeDtypeStruct((T, qk_cols), BF16),
                   jax.ShapeDtypeStruct((T, ATT_WIDTH), BF16), jax.ShapeDtypeStruct((T, conv_ch), BF16)],
        compiler_params=_params("arbitrary"),
        name="in_proj",
    )(x2, g1, w_in, qg, kg, cos, sin)


def _attn_kernel(lq1_ref, lk1_ref, lq2_ref, lk2_ref, sg_ref, q_ref, k_ref, v_ref, o_ref,
                 acc1_ref, acc2_ref, *, tq, lam_init):
    i = pl.program_id(2)
    q = q_ref[...]
    q1, q2 = q[:, :HEAD_DIM], q[:, HEAD_DIM:]
    nt = (((1,), (1,)), ((), ()))

    def scores(j):
        kb = k_ref[pl.ds(pl.multiple_of(j * tq, tq), tq), :]
        s1 = lax.dot_general(q1, kb[:, :HEAD_DIM], nt, preferred_element_type=F32)
        s2 = lax.dot_general(q2, kb[:, HEAD_DIM:], nt, preferred_element_type=F32)
        return s1, s2

    def pv(p, j):
        vb = v_ref[pl.ds(pl.multiple_of(j * tq, tq), tq), :]
        return jnp.dot(p.astype(BF16), vb, preferred_element_type=F32)

    rows = lax.broadcasted_iota(jnp.int32, (tq, tq), 0) // CHUNK
    cols = lax.broadcasted_iota(jnp.int32, (tq, tq), 1) // CHUNK
    visible = cols <= rows
    s1, s2 = scores(i)
    init = []
    for s, acc_ref in ((s1, acc1_ref), (s2, acc2_ref)):
        s = jnp.where(visible, s, -jnp.inf)
        m = jnp.max(s, axis=-1, keepdims=True)
        p = jnp.exp(s - m)
        acc_ref[...] = pv(p, i)
        init += [m, jnp.sum(p, axis=-1, keepdims=True)]

    def body(j, carry):
        out = []
        for s, acc_ref, m, l in zip(scores(j), (acc1_ref, acc2_ref), carry[0::2], carry[1::2]):
            m_new = jnp.maximum(m, jnp.max(s, axis=-1, keepdims=True))
            a = jnp.exp(m - m_new)
            p = jnp.exp(s - m_new)
            acc_ref[...] = a * acc_ref[...] + pv(p, j)
            out += [m_new, a * l + jnp.sum(p, axis=-1, keepdims=True)]
        return tuple(out)

    _, l1, _, l2 = lax.fori_loop(0, i, body, tuple(init))

    lam = (jnp.exp(jnp.sum(lq1_ref[...] * lk1_ref[...], keepdims=True))
           - jnp.exp(jnp.sum(lq2_ref[...] * lk2_ref[...], keepdims=True)) + lam_init)
    o = acc1_ref[...] * (1.0 / l1) - lam * (acc2_ref[...] * (1.0 / l2))
    o_ref[...] = (_rms(o) * sg_ref[...] * (1.0 - lam_init)).astype(BF16)


def _attention(lams, sg, q, k, v, *, batch, seq, tq, lam_init):
    q3, k3, v3 = (a.reshape(batch, seq, a.shape[-1]) for a in (q, k, v))
    blk = pl.BlockSpec((None, tq, V_DIM), lambda b, h, i: (b, i, h))
    whole = pl.BlockSpec((None, seq, V_DIM), lambda b, h, i: (b, 0, h))
    out = pl.pallas_call(
        functools.partial(_attn_kernel, tq=tq, lam_init=lam_init),
        grid=(batch, ATT_HEADS, seq // tq),
        in_specs=[_resident((1, HEAD_DIM))] * 4 + [_resident((1, V_DIM)), blk, whole, whole],
        out_specs=blk,
        out_shape=jax.ShapeDtypeStruct((batch, seq, ATT_WIDTH), BF16),
        scratch_shapes=[pltpu.VMEM((tq, V_DIM), F32), pltpu.VMEM((tq, V_DIM), F32)],
        compiler_params=_params("arbitrary", "arbitrary", "arbitrary"),
        name="diff_attn",
    )(*lams, sg, q3, k3, v3)
    return out.reshape(batch * seq, ATT_WIDTH)


def _conv_kernel(u_ref, halo_ref, w_ref, b_ref, lg_ref, lb_ref, o_ref, ext_ref, conv_ref, *, ts, rc, rn):
    C = u_ref.shape[-1]
    halo = halo_ref[...].astype(F32)
    ext_ref[0:CONV_HALO, :] = jnp.where(pl.program_id(1) == 0, 0.0, halo)
    ext_ref[CONV_HALO:CONV_HALO + ts, :] = u_ref[...].astype(F32)
    ext_ref[CONV_HALO + ts:, :] = jnp.zeros((SUBLANES, C), F32)
    first = CONV_HALO - (CONV_WIDTH - 1)

    def conv_chunk(r, _):
        r0 = pl.multiple_of(r * rc, rc)
        for c0 in range(0, C, LANES):
            win = ext_ref[pl.ds(r0, rc + CONV_HALO + SUBLANES), c0:c0 + LANES]
            acc = jnp.broadcast_to(b_ref[:, c0:c0 + LANES], (rc, LANES))
            for res in range(SUBLANES):
                shifted = win[res:res + rc + CONV_HALO, :]
                for a in range(CONV_HALO // SUBLANES + 1):
                    t = a * SUBLANES + res - first
                    if 0 <= t < CONV_WIDTH:
                        acc = acc + shifted[a * SUBLANES:a * SUBLANES + rc, :] * w_ref[t:t + 1, c0:c0 + LANES]
            conv_ref[pl.ds(r0, rc), c0:c0 + LANES] = acc
        return 0

    lax.fori_loop(0, ts // rc, conv_chunk, 0)

    def norm_chunk(r, _):
        r0 = pl.multiple_of(r * rn, rn)
        acc = conv_ref[pl.ds(r0, rn), :]
        mu = jnp.mean(acc, axis=-1, keepdims=True)
        d = acc - mu
        y = d * lax.rsqrt(jnp.mean(d * d, axis=-1, keepdims=True) + LN_EPS) * lg_ref[...] + lb_ref[...]
        o_ref[pl.ds(r0, rn), :] = (y * jax.nn.sigmoid(y)).astype(BF16)
        return 0

    lax.fori_loop(0, ts // rn, norm_chunk, 0)


def _conv(u, w, b, lg, lb, *, batch, seq, ts, rc=64, rn=16):
    C = u.shape[-1]
    u3 = u.reshape(batch, seq, C)
    per_tile = ts // CONV_HALO
    out = pl.pallas_call(
        functools.partial(_conv_kernel, ts=ts, rc=rc, rn=rn),
        grid=(batch, seq // ts),
        in_specs=[pl.BlockSpec((None, ts, C), lambda bb, i: (bb, i, 0)),
                  pl.BlockSpec((None, CONV_HALO, C), lambda bb, i: (bb, jnp.maximum(i * per_tile - 1, 0), 0)),
                  _resident((CONV_WIDTH, C)), _resident((1, C)), _resident((1, C)), _resident((1, C))],
        out_specs=pl.BlockSpec((None, ts, C), lambda bb, i: (bb, i, 0)),
        out_shape=jax.ShapeDtypeStruct((batch, seq, C), BF16),
        scratch_shapes=[pltpu.VMEM((ts + CONV_HALO + SUBLANES, C), F32), pltpu.VMEM((ts, C), F32)],
        compiler_params=_params("arbitrary", "arbitrary"),
        name="conv_ln_silu",
    )(u3, u3, w, b, lg, lb)
    return out.reshape(batch * seq, C)


def _out_proj_kernel(x_ref, att_ref, uc_ref, w_ref, h_ref, *, att_w):
    h = x_ref[...] + jnp.dot(att_ref[...], w_ref[:att_w, :], preferred_element_type=F32)
    h_ref[...] = h + jnp.dot(uc_ref[...], w_ref[att_w:, :], preferred_element_type=F32)


def _out_proj(x2, att, uc, w_out, *, tm):
    T, D = x2.shape
    row = lambda w: pl.BlockSpec((tm, w), lambda i: (i, 0))
    return pl.pallas_call(
        functools.partial(_out_proj_kernel, att_w=att.shape[1]),
        grid=(T // tm,),
        in_specs=[row(D), row(att.shape[1]), row(uc.shape[1]), _resident(w_out.shape)],
        out_specs=row(D),
        out_shape=jax.ShapeDtypeStruct((T, D), F32),
        compiler_params=_params("arbitrary"),
        name="out_proj",
    )(x2, att, uc, w_out)


def _ffn_kernel(h_ref, g2_ref, wg_ref, wu_ref, wd_ref, o_ref, hn_ref):
    f = pl.program_id(1)

    @pl.when(f == 0)
    def _():
        hn_ref[...] = (_rms(h_ref[...]) * g2_ref[...]).astype(BF16)

    hn = hn_ref[...]
    a = jnp.dot(hn, wg_ref[...], preferred_element_type=F32)
    b = jnp.dot(hn, wu_ref[...], preferred_element_type=F32)
    g = (a * jax.nn.sigmoid(a) * b).astype(BF16)
    y = jnp.dot(g, wd_ref[...], preferred_element_type=F32)

    @pl.when(f == 0)
    def _():
        o_ref[...] = h_ref[...] + y

    @pl.when(f != 0)
    def _():
        o_ref[...] += y


def _ffn(h, g2, wg, wu, wd, *, tm, tf):
    T, D = h.shape
    F = wg.shape[1]
    return pl.pallas_call(
        _ffn_kernel,
        grid=(T // tm, F // tf),
        in_specs=[pl.BlockSpec((tm, D), lambda i, f: (i, 0)), _resident((1, D)),
                  pl.BlockSpec((D, tf), lambda i, f: (0, f)), pl.BlockSpec((D, tf), lambda i, f: (0, f)),
                  pl.BlockSpec((tf, D), lambda i, f: (f, 0))],
        out_specs=pl.BlockSpec((tm, D), lambda i, f: (i, 0)),
        out_shape=jax.ShapeDtypeStruct((T, D), F32),
        scratch_shapes=[pltpu.VMEM((tm, D), BF16)],
        compiler_params=_params("arbitrary", "arbitrary"),
        name="swiglu_ffn",
    )(h, g2, wg, wu, wd)


def _tile(n, want):
    t = want
    while n % t:
        t //= 2
    return t


def _rope_tables(seq):
    inv_freq = ROPE_THETA ** (-jnp.arange(0, HEAD_DIM, 2, dtype=F32) / HEAD_DIM)
    ang = jnp.arange(seq, dtype=F32)[:, None] * inv_freq[None, :]
    ang = jnp.concatenate([ang, ang], axis=-1)
    sign = jnp.where(jnp.arange(HEAD_DIM) < HEAD_DIM // 2, -1.0, 1.0).astype(F32)
    return jnp.cos(ang), jnp.sin(ang) * sign


def kernel(x, norm1_g, w_in, q_norm_g, k_norm_g, lambda_q1, lambda_k1, lambda_q2, lambda_k2, subln_g,
           conv_w, conv_b, conv_ln_g, conv_ln_b, w_out, norm2_g, w_gate, w_up, w_down):
    B, S, D = x.shape
    T = B * S
    F = w_gate.shape[-1]
    cos, sin = _rope_tables(S)
    row = lambda a: a.reshape(1, -1).astype(F32)
    h = x.reshape(T, D)
    for l in range(w_in.shape[0]):
        lam_init = 0.8 - 0.6 * math.exp(-0.3 * l)
        q, k, v, u = _in_proj(h, row(norm1_g[l]), w_in[l].astype(BF16), row(q_norm_g[l]), row(k_norm_g[l]),
                              cos, sin, seq=S, tm=_tile(S, 512))
        lams = [row(a[l]) for a in (lambda_q1, lambda_k1, lambda_q2, lambda_k2)]
        att = _attention(lams, row(subln_g[l]), q, k, v, batch=B, seq=S, tq=_tile(S, 512), lam_init=lam_init)
        uc = _conv(u, conv_w[l].reshape(CONV_WIDTH, -1).astype(F32), row(conv_b[l]), row(conv_ln_g[l]),
                   row(conv_ln_b[l]), batch=B, seq=S, ts=_tile(S, 256))
        h = _out_proj(h, att, uc, w_out[l].astype(BF16), tm=_tile(T, 512))
        h = _ffn(h, row(norm2_g[l]), w_gate[l].astype(BF16), w_up[l].astype(BF16), w_down[l].astype(BF16),
                 tm=_tile(T, 1024), tf=_tile(F, 512))
    return h.reshape(B, S, D)
```

```python
import functools
import math

import jax
import jax.numpy as jnp
from jax import lax
from jax.experimental import pallas as pl
from jax.experimental.pallas import tpu as pltpu

ATT_HEADS = 4
HEAD_DIM = 128
V_DIM = 2 * HEAD_DIM
ATT_WIDTH = ATT_HEADS * V_DIM
CHUNK = 64
CONV_WIDTH = 31
CONV_HALO = 32
ROPE_THETA = 10000.0
EPS = 1e-6
LN_EPS = 1e-5
LOG2_E = 1.4426950408889634
SAFE_SCORE = 30.0

SUBLANES = 8
LANES = 128
VMEM_LIMIT_BYTES = 58 * 1024 * 1024

F32 = jnp.float32
BF16 = jnp.bfloat16


def _params(*semantics):
    return pltpu.CompilerParams(dimension_semantics=semantics, vmem_limit_bytes=VMEM_LIMIT_BYTES)


def _resident(shape):
    return pl.BlockSpec(shape, lambda *_: (0,) * len(shape), pipeline_mode=pl.Buffered(1))


def _rms(t, eps=EPS):
    return t * lax.rsqrt(jnp.mean(t * t, axis=-1, keepdims=True) + eps)


def _in_proj_kernel(x_ref, g1_ref, w_ref, qg_ref, kg_ref, cos_ref, sin_ref,
                    q_ref, k_ref, v_ref, u_ref, *, qk_cols, att_w, conv_ch):
    xn = (_rms(x_ref[...]) * g1_ref[...]).astype(BF16)
    cos = cos_ref[...]
    sin = sin_ref[...]

    def proj(c0, width=V_DIM):
        return jnp.dot(xn, w_ref[:, c0:c0 + width], preferred_element_type=F32)

    def norm_rope(t, g, scale):
        y = _rms(t) * g
        y = y * cos + pltpu.roll(y, HEAD_DIM // 2, 1) * sin
        return (y * scale).astype(BF16)

    for o_ref, g_ref, base, scale in ((q_ref, qg_ref, 0, LOG2_E / math.sqrt(HEAD_DIM)),
                                      (k_ref, kg_ref, qk_cols, 1.0)):
        for h in range(qk_cols // V_DIM):
            t = proj(base + h * V_DIM)
            for c in range(2):
                lo = h * V_DIM + c * HEAD_DIM
                o_ref[:, lo:lo + HEAD_DIM] = norm_rope(t[:, c * HEAD_DIM:(c + 1) * HEAD_DIM], g_ref[...], scale)
    for h in range(att_w // V_DIM):
        v_ref[:, h * V_DIM:(h + 1) * V_DIM] = proj(2 * qk_cols + h * V_DIM).astype(BF16)
    for j in range(conv_ch // V_DIM):
        ga = proj(2 * qk_cols + att_w + j * V_DIM)
        gg = proj(2 * qk_cols + att_w + conv_ch + j * V_DIM)
        u_ref[:, j * V_DIM:(j + 1) * V_DIM] = (ga * jax.nn.sigmoid(gg)).astype(BF16)


def _in_proj(x2, g1, w_in, qg, kg, cos, sin, *, seq, tm):
    T, D = x2.shape
    qk_cols = ATT_HEADS * 2 * HEAD_DIM
    conv_ch = (w_in.shape[1] - 2 * qk_cols - ATT_WIDTH) // 2
    n_pos = seq // tm
    row = lambda w: pl.BlockSpec((tm, w), lambda i: (i, 0))
    pos = pl.BlockSpec((tm, HEAD_DIM), lambda i: (i % n_pos, 0))
    return pl.pallas_call(
        functools.partial(_in_proj_kernel, qk_cols=qk_cols, att_w=ATT_WIDTH, conv_ch=conv_ch),
        grid=(T // tm,),
        in_specs=[row(D), _resident((1, D)), _resident(w_in.shape), _resident((1, HEAD_DIM)),
                  _resident((1, HEAD_DIM)), pos, pos],
        out_specs=[row(qk_cols), row(qk_cols), row(ATT_WIDTH), row(conv_ch)],
        out_shape=[jax.ShapeDtypeStruct((T, qk_cols), BF16), jax.ShapeDtypeStruct((T, qk_cols), BF16),
                   jax.ShapeDtypeStruct((T, ATT_WIDTH), BF16), jax.ShapeDtypeStruct((T, conv_ch), BF16)],
        compiler_params=_params("arbitrary"),
        name="in_proj",
    )(x2, g1, w_in, qg, kg, cos, sin)


def _attn_kernel(bounded_ref, lq1_ref, lk1_ref, lq2_ref, lk2_ref, sg_ref, q_ref, k_ref, v_ref, o_ref,
                 acc1_ref, acc2_ref, *, tq, lam_init):
    i = pl.program_id(2)
    q = q_ref[...]
    q1, q2 = q[:, :HEAD_DIM], q[:, HEAD_DIM:]
    nt = (((1,), (1,)), ((), ()))
    accs = (acc1_ref, acc2_ref)

    def scores(j):
        kb = k_ref[pl.ds(pl.multiple_of(j * tq, tq), tq), :]
        s1 = lax.dot_general(q1, kb[:, :HEAD_DIM], nt, preferred_element_type=F32)
        s2 = lax.dot_general(q2, kb[:, HEAD_DIM:], nt, preferred_element_type=F32)
        return s1, s2

    def pv(p, j):
        vb = v_ref[pl.ds(pl.multiple_of(j * tq, tq), tq), :]
        return jnp.dot(p.astype(BF16), vb, preferred_element_type=F32)

    def rowsum(p):
        return jnp.sum(p, axis=-1, keepdims=True)

    def finish(l1, l2):
        lam = (jnp.exp(jnp.sum(lq1_ref[...] * lk1_ref[...], keepdims=True))
               - jnp.exp(jnp.sum(lq2_ref[...] * lk2_ref[...], keepdims=True)) + lam_init)
        o = acc1_ref[...] * (1.0 / l1) - lam * (acc2_ref[...] * (1.0 / l2))
        o_ref[...] = (_rms(o) * sg_ref[...] * (1.0 - lam_init)).astype(BF16)

    rows = lax.broadcasted_iota(jnp.int32, (tq, tq), 0) // CHUNK
    cols = lax.broadcasted_iota(jnp.int32, (tq, tq), 1) // CHUNK
    visible = cols <= rows

    @pl.when(bounded_ref[0] != 0)
    def _():
        init = []
        for s, acc_ref in zip(scores(i), accs):
            p = jnp.where(visible, jnp.exp2(s), 0.0)
            acc_ref[...] = pv(p, i)
            init.append(rowsum(p))

        def body(j, ls):
            out = []
            for s, acc_ref, l in zip(scores(j), accs, ls):
                p = jnp.exp2(s)
                acc_ref[...] += pv(p, j)
                out.append(l + rowsum(p))
            return tuple(out)

        finish(*lax.fori_loop(0, i, body, tuple(init)))

    @pl.when(bounded_ref[0] == 0)
    def _():
        init = []
        for s, acc_ref in zip(scores(i), accs):
            s = jnp.where(visible, s, -jnp.inf)
            m = jnp.max(s, axis=-1, keepdims=True)
            p = jnp.exp2(s - m)
            acc_ref[...] = pv(p, i)
            init += [m, rowsum(p)]

        def body(j, carry):
            out = []
            for s, acc_ref, m, l in zip(scores(j), accs, carry[0::2], carry[1::2]):
                m_new = jnp.maximum(m, jnp.max(s, axis=-1, keepdims=True))
                a = jnp.exp2(m - m_new)
                p = jnp.exp2(s - m_new)
                acc_ref[...] = a * acc_ref[...] + pv(p, j)
                out += [m_new, a * l + rowsum(p)]
            return tuple(out)

        _, l1, _, l2 = lax.fori_loop(0, i, body, tuple(init))
        finish(l1, l2)


def _attention(bounded, lams, sg, q, k, v, *, batch, seq, tq, lam_init):
    q3, k3, v3 = (a.reshape(batch, seq, a.shape[-1]) for a in (q, k, v))
    blk = pl.BlockSpec((None, tq, V_DIM), lambda b, h, i: (b, i, h))
    whole = pl.BlockSpec((None, seq, V_DIM), lambda b, h, i: (b, 0, h))
    out = pl.pallas_call(
        functools.partial(_attn_kernel, tq=tq, lam_init=lam_init),
        grid=(batch, ATT_HEADS, seq // tq),
        in_specs=[pl.BlockSpec(memory_space=pltpu.SMEM)] + [_resident((1, HEAD_DIM))] * 4
                 + [_resident((1, V_DIM)), blk, whole, whole],
        out_specs=blk,
        out_shape=jax.ShapeDtypeStruct((batch, seq, ATT_WIDTH), BF16),
        scratch_shapes=[pltpu.VMEM((tq, V_DIM), F32), pltpu.VMEM((tq, V_DIM), F32)],
        compiler_params=_params("arbitrary", "arbitrary", "arbitrary"),
        name="diff_attn",
    )(bounded, *lams, sg, q3, k3, v3)
    return out.reshape(batch * seq, ATT_WIDTH)


def _conv_kernel(u_ref, halo_ref, w_ref, b_ref, lg_ref, lb_ref, o_ref, ext_ref, conv_ref, *, ts, rc, rn):
    C = u_ref.shape[-1]
    halo = halo_ref[...].astype(F32)
    ext_ref[0:CONV_HALO, :] = jnp.where(pl.program_id(1) == 0, 0.0, halo)
    ext_ref[CONV_HALO:CONV_HALO + ts, :] = u_ref[...].astype(F32)
    ext_ref[CONV_HALO + ts:, :] = jnp.zeros((SUBLANES, C), F32)
    first = CONV_HALO - (CONV_WIDTH - 1)

    def conv_chunk(r, _):
        r0 = pl.multiple_of(r * rc, rc)
        for c0 in range(0, C, LANES):
            win = ext_ref[pl.ds(r0, rc + CONV_HALO + SUBLANES), c0:c0 + LANES]
            acc = jnp.broadcast_to(b_ref[:, c0:c0 + LANES], (rc, LANES))
            for res in range(SUBLANES):
                part = None
                for a in range(CONV_HALO // SUBLANES + 1):
                    t = a * SUBLANES + res - first
                    if 0 <= t < CONV_WIDTH:
                        lo = a * SUBLANES
                        term = win[lo:lo + rc + SUBLANES, :] * w_ref[t:t + 1, c0:c0 + LANES]
                        part = term if part is None else part + term
                acc = acc + part[res:res + rc, :]
            conv_ref[pl.ds(r0, rc), c0:c0 + LANES] = acc
        return 0

    lax.fori_loop(0, ts // rc, conv_chunk, 0)

    for r0 in range(0, ts, rn):
        acc = conv_ref[r0:r0 + rn, :]
        mu = jnp.mean(acc, axis=-1, keepdims=True)
        d = acc - mu
        y = d * lax.rsqrt(jnp.mean(d * d, axis=-1, keepdims=True) + LN_EPS) * lg_ref[...] + lb_ref[...]
        o_ref[r0:r0 + rn, :] = (y * jax.nn.sigmoid(y)).astype(BF16)


def _conv(u, w, b, lg, lb, *, batch, seq, ts, rc=64, rn=32):
    C = u.shape[-1]
    u3 = u.reshape(batch, seq, C)
    per_tile = ts // CONV_HALO
    out = pl.pallas_call(
        functools.partial(_conv_kernel, ts=ts, rc=rc, rn=rn),
        grid=(batch, seq // ts),
        in_specs=[pl.BlockSpec((None, ts, C), lambda bb, i: (bb, i, 0)),
                  pl.BlockSpec((None, CONV_HALO, C), lambda bb, i: (bb, jnp.maximum(i * per_tile - 1, 0), 0)),
                  _resident((CONV_WIDTH, C)), _resident((1, C)), _resident((1, C)), _resident((1, C))],
        out_specs=pl.BlockSpec((None, ts, C), lambda bb, i: (bb, i, 0)),
        out_shape=jax.ShapeDtypeStruct((batch, seq, C), BF16),
        scratch_shapes=[pltpu.VMEM((ts + CONV_HALO + SUBLANES, C), F32), pltpu.VMEM((ts, C), F32)],
        compiler_params=_params("arbitrary", "arbitrary"),
        name="conv_ln_silu",
    )(u3, u3, w, b, lg, lb)
    return out.reshape(batch * seq, C)


def _out_proj_kernel(x_ref, att_ref, uc_ref, w_ref, h_ref, *, att_w):
    h = x_ref[...] + jnp.dot(att_ref[...], w_ref[:att_w, :], preferred_element_type=F32)
    h_ref[...] = h + jnp.dot(uc_ref[...], w_ref[att_w:, :], preferred_element_type=F32)


def _out_proj(x2, att, uc, w_out, *, tm):
    T, D = x2.shape
    row = lambda w: pl.BlockSpec((tm, w), lambda i: (i, 0))
    return pl.pallas_call(
        functools.partial(_out_proj_kernel, att_w=att.shape[1]),
        grid=(T // tm,),
        in_specs=[row(D), row(att.shape[1]), row(uc.shape[1]), _resident(w_out.shape)],
        out_specs=row(D),
        out_shape=jax.ShapeDtypeStruct((T, D), F32),
        compiler_params=_params("arbitrary"),
        name="out_proj",
    )(x2, att, uc, w_out)


def _ffn_kernel(h_ref, g2_ref, wg_ref, wu_ref, wd_ref, o_ref, hn_ref):
    f = pl.program_id(1)

    @pl.when(f == 0)
    def _():
        hn_ref[...] = (_rms(h_ref[...]) * g2_ref[...]).astype(BF16)

    hn = hn_ref[...]
    a = jnp.dot(hn, wg_ref[...], preferred_element_type=F32)
    b = jnp.dot(hn, wu_ref[...], preferred_element_type=F32)
    g = (a * jax.nn.sigmoid(a) * b).astype(BF16)
    y = jnp.dot(g, wd_ref[...], preferred_element_type=F32)

    @pl.when(f == 0)
    def _():
        o_ref[...] = h_ref[...] + y

    @pl.when(f != 0)
    def _():
        o_ref[...] += y


def _ffn(h, g2, wg, wu, wd, *, tm, tf):
    T, D = h.shape
    F = wg.shape[1]
    return pl.pallas_call(
        _ffn_kernel,
        grid=(T // tm, F // tf),
        in_specs=[pl.BlockSpec((tm, D), lambda i, f: (i, 0)), _resident((1, D)),
                  pl.BlockSpec((D, tf), lambda i, f: (0, f)), pl.BlockSpec((D, tf), lambda i, f: (0, f)),
                  pl.BlockSpec((tf, D), lambda i, f: (f, 0))],
        out_specs=pl.BlockSpec((tm, D), lambda i, f: (i, 0)),
        out_shape=jax.ShapeDtypeStruct((T, D), F32),
        scratch_shapes=[pltpu.VMEM((tm, D), BF16)],
        compiler_params=_params("arbitrary", "arbitrary"),
        name="swiglu_ffn",
    )(h, g2, wg, wu, wd)


def _tile(n, want):
    t = want
    while n % t:
        t //= 2
    return t


def _rope_tables(seq):
    inv_freq = ROPE_THETA ** (-jnp.arange(0, HEAD_DIM, 2, dtype=F32) / HEAD_DIM)
    ang = jnp.arange(seq, dtype=F32)[:, None] * inv_freq[None, :]
    ang = jnp.concatenate([ang, ang], axis=-1)
    sign = jnp.where(jnp.arange(HEAD_DIM) < HEAD_DIM // 2, -1.0, 1.0).astype(F32)
    return jnp.cos(ang), jnp.sin(ang) * sign


def kernel(x, norm1_g, w_in, q_norm_g, k_norm_g, lambda_q1, lambda_k1, lambda_q2, lambda_k2, subln_g,
           conv_w, conv_b, conv_ln_g, conv_ln_b, w_out, norm2_g, w_gate, w_up, w_down):
    B, S, D = x.shape
    T = B * S
    F = w_gate.shape[-1]
    cos, sin = _rope_tables(S)
    row = lambda a: a.reshape(1, -1).astype(F32)
    h = x.reshape(T, D)
    for l in range(w_in.shape[0]):
        lam_init = 0.8 - 0.6 * math.exp(-0.3 * l)
        q, k, v, u = _in_proj(h, row(norm1_g[l]), w_in[l].astype(BF16), row(q_norm_g[l]), row(k_norm_g[l]),
                              cos, sin, seq=S, tm=_tile(S, 512))
        lams = [row(a[l]) for a in (lambda_q1, lambda_k1, lambda_q2, lambda_k2)]
        score_bound = 1.02 * math.sqrt(HEAD_DIM) * jnp.max(jnp.abs(q_norm_g[l])) * jnp.max(jnp.abs(k_norm_g[l]))
        bounded = (score_bound <= SAFE_SCORE).astype(jnp.int32).reshape(1)
        att = _attention(bounded, lams, row(subln_g[l]), q, k, v, batch=B, seq=S, tq=_tile(S, 1024),
                         lam_init=lam_init)
        uc = _conv(u, conv_w[l].reshape(CONV_WIDTH, -1).astype(F32), row(conv_b[l]), row(conv_ln_g[l]),
                   row(conv_ln_b[l]), batch=B, seq=S, ts=_tile(S, 256))
        h = _out_proj(h, att, uc, w_out[l].astype(BF16), tm=_tile(T, 512))
        h = _ffn(h, row(norm2_g[l]), w_gate[l].astype(BF16), w_up[l].astype(BF16), w_down[l].astype(BF16),
                 tm=_tile(T, 1024), tf=_tile(F, 512))
    return h.reshape(B, S, D)
```

```python
import functools
import math

import jax
import jax.numpy as jnp
from jax import lax
from jax.experimental import pallas as pl
from jax.experimental.pallas import tpu as pltpu

ATT_HEADS = 4
HEAD_DIM = 128
V_DIM = 2 * HEAD_DIM
ATT_WIDTH = ATT_HEADS * V_DIM
CHUNK = 64
CONV_WIDTH = 31
CONV_HALO = 32
ROPE_THETA = 10000.0
EPS = 1e-6
LN_EPS = 1e-5
LOG2_E = 1.4426950408889634
SAFE_SCORE = 30.0

SUBLANES = 8
LANES = 128
VMEM_LIMIT_BYTES = 58 * 1024 * 1024

F32 = jnp.float32
BF16 = jnp.bfloat16


def _params(*semantics):
    return pltpu.CompilerParams(dimension_semantics=semantics, vmem_limit_bytes=VMEM_LIMIT_BYTES)


def _resident(shape):
    return pl.BlockSpec(shape, lambda *_: (0,) * len(shape), pipeline_mode=pl.Buffered(1))


def _rms(t, eps=EPS):
    return t * lax.rsqrt(jnp.mean(t * t, axis=-1, keepdims=True) + eps)


def _in_proj_kernel(x_ref, g1_ref, w_ref, qg_ref, kg_ref, cos_ref, sin_ref,
                    q_ref, k_ref, v_ref, u_ref, *, qk_cols, att_w, conv_ch):
    xn = (_rms(x_ref[...]) * g1_ref[...]).astype(BF16)
    cos = cos_ref[...]
    sin = sin_ref[...]

    def proj(c0, width=V_DIM):
        return jnp.dot(xn, w_ref[:, c0:c0 + width], preferred_element_type=F32)

    def norm_rope(t, g, scale):
        y = _rms(t) * g
        y = y * cos + pltpu.roll(y, HEAD_DIM // 2, 1) * sin
        return (y * scale).astype(BF16)

    for o_ref, g_ref, base, scale in ((q_ref, qg_ref, 0, LOG2_E / math.sqrt(HEAD_DIM)),
                                      (k_ref, kg_ref, qk_cols, 1.0)):
        for h in range(qk_cols // V_DIM):
            t = proj(base + h * V_DIM)
            for c in range(2):
                lo = h * V_DIM + c * HEAD_DIM
                o_ref[:, lo:lo + HEAD_DIM] = norm_rope(t[:, c * HEAD_DIM:(c + 1) * HEAD_DIM], g_ref[...], scale)
    for h in range(att_w // V_DIM):
        v_ref[:, h * V_DIM:(h + 1) * V_DIM] = proj(2 * qk_cols + h * V_DIM).astype(BF16)
    for j in range(conv_ch // V_DIM):
        ga = proj(2 * qk_cols + att_w + j * V_DIM)
        gg = proj(2 * qk_cols + att_w + conv_ch + j * V_DIM)
        u_ref[:, j * V_DIM:(j + 1) * V_DIM] = (ga * jax.nn.sigmoid(gg)).astype(BF16)


def _in_proj(x2, g1, w_in, qg, kg, cos, sin, *, seq, tm):
    T, D = x2.shape
    qk_cols = ATT_HEADS * 2 * HEAD_DIM
    conv_ch = (w_in.shape[1] - 2 * qk_cols - ATT_WIDTH) // 2
    n_pos = seq // tm
    row = lambda w: pl.BlockSpec((tm, w), lambda i: (i, 0))
    pos = pl.BlockSpec((tm, HEAD_DIM), lambda i: (i % n_pos, 0))
    return pl.pallas_call(
        functools.partial(_in_proj_kernel, qk_cols=qk_cols, att_w=ATT_WIDTH, conv_ch=conv_ch),
        grid=(T // tm,),
        in_specs=[row(D), _resident((1, D)), _resident(w_in.shape), _resident((1, HEAD_DIM)),
                  _resident((1, HEAD_DIM)), pos, pos],
        out_specs=[row(qk_cols), row(qk_cols), row(ATT_WIDTH), row(conv_ch)],
        out_shape=[jax.ShapeDtypeStruct((T, qk_cols), BF16), jax.ShapeDtypeStruct((T, qk_cols), BF16),
                   jax.ShapeDtypeStruct((T, ATT_WIDTH), BF16), jax.ShapeDtypeStruct((T, conv_ch), BF16)],
        compiler_params=_params("arbitrary"),
        name="in_proj",
    )(x2, g1, w_in, qg, kg, cos, sin)


def _attn_kernel(bounded_ref, lq1_ref, lk1_ref, lq2_ref, lk2_ref, sg_ref, q_ref, k_ref, v_ref, o_ref,
                 acc1_ref, acc2_ref, *, tq, lam_init):
    i = pl.program_id(2)
    hq = tq // 2
    nt = (((1,), (1,)), ((), ()))
    accs = (acc1_ref, acc2_ref)

    def block(ref, j):
        return ref[pl.ds(pl.multiple_of(j * tq, tq), tq), :]

    def qk(qb, kb):
        return (lax.dot_general(qb[:, :HEAD_DIM], kb[:, :HEAD_DIM], nt, preferred_element_type=F32),
                lax.dot_general(qb[:, HEAD_DIM:], kb[:, HEAD_DIM:], nt, preferred_element_type=F32))

    def pv(p, vb):
        return jnp.dot(p.astype(BF16), vb, preferred_element_type=F32)

    def rowsum(p):
        return jnp.sum(p, axis=-1, keepdims=True)

    def visible(n_rows, n_cols, row0):
        rows = (lax.broadcasted_iota(jnp.int32, (n_rows, n_cols), 0) + row0) // CHUNK
        cols = lax.broadcasted_iota(jnp.int32, (n_rows, n_cols), 1) // CHUNK
        return cols <= rows

    def finish(l1, l2):
        lam = (jnp.exp(jnp.sum(lq1_ref[...] * lk1_ref[...], keepdims=True))
               - jnp.exp(jnp.sum(lq2_ref[...] * lk2_ref[...], keepdims=True)) + lam_init)
        o = acc1_ref[...] * (1.0 / l1) - lam * (acc2_ref[...] * (1.0 / l2))
        o_ref[...] = (_rms(o) * sg_ref[...] * (1.0 - lam_init)).astype(BF16)

    @pl.when(bounded_ref[0] != 0)
    def _():
        kd, vd = block(k_ref, i), block(v_ref, i)
        top = qk(q_ref[:hq, :], kd[:hq, :])
        bot = qk(q_ref[hq:, :], kd)
        vis_top, vis_bot = visible(hq, hq, 0), visible(hq, tq, hq)
        init = []
        for s_top, s_bot, acc_ref in zip(top, bot, accs):
            p_top = jnp.where(vis_top, jnp.exp2(s_top), 0.0)
            p_bot = jnp.where(vis_bot, jnp.exp2(s_bot), 0.0)
            acc_ref[:hq, :] = pv(p_top, vd[:hq, :])
            acc_ref[hq:, :] = pv(p_bot, vd)
            init.append(jnp.concatenate([rowsum(p_top), rowsum(p_bot)], axis=0))

        def body(j, ls):
            out = []
            for s, acc_ref, l in zip(qk(q_ref[...], block(k_ref, j)), accs, ls):
                p = jnp.exp2(s)
                acc_ref[...] += pv(p, block(v_ref, j))
                out.append(l + rowsum(p))
            return tuple(out)

        finish(*lax.fori_loop(0, i, body, tuple(init)))

    @pl.when(bounded_ref[0] == 0)
    def _():
        q = q_ref[...]
        vis = visible(tq, tq, 0)

        def scores(j):
            return qk(q, block(k_ref, j))

        init = []
        for s, acc_ref in zip(scores(i), accs):
            s = jnp.where(vis, s, -jnp.inf)
            m = jnp.max(s, axis=-1, keepdims=True)
            p = jnp.exp2(s - m)
            acc_ref[...] = pv(p, block(v_ref, i))
            init += [m, rowsum(p)]

        def body(j, carry):
            out = []
            for s, acc_ref, m, l in zip(scores(j), accs, carry[0::2], carry[1::2]):
                m_new = jnp.maximum(m, jnp.max(s, axis=-1, keepdims=True))
                a = jnp.exp2(m - m_new)
                p = jnp.exp2(s - m_new)
                acc_ref[...] = a * acc_ref[...] + pv(p, block(v_ref, j))
                out += [m_new, a * l + rowsum(p)]
            return tuple(out)

        _, l1, _, l2 = lax.fori_loop(0, i, body, tuple(init))
        finish(l1, l2)


def _attention(bounded, lams, sg, q, k, v, *, batch, seq, tq, lam_init):
    q3, k3, v3 = (a.reshape(batch, seq, a.shape[-1]) for a in (q, k, v))
    blk = pl.BlockSpec((None, tq, V_DIM), lambda b, h, i: (b, i, h))
    whole = pl.BlockSpec((None, seq, V_DIM), lambda b, h, i: (b, 0, h))
    out = pl.pallas_call(
        functools.partial(_attn_kernel, tq=tq, lam_init=lam_init),
        grid=(batch, ATT_HEADS, seq // tq),
        in_specs=[pl.BlockSpec(memory_space=pltpu.SMEM)] + [_resident((1, HEAD_DIM))] * 4
                 + [_resident((1, V_DIM)), blk, whole, whole],
        out_specs=blk,
        out_shape=jax.ShapeDtypeStruct((batch, seq, ATT_WIDTH), BF16),
        scratch_shapes=[pltpu.VMEM((tq, V_DIM), F32), pltpu.VMEM((tq, V_DIM), F32)],
        compiler_params=_params("arbitrary", "arbitrary", "arbitrary"),
        name="diff_attn",
    )(bounded, *lams, sg, q3, k3, v3)
    return out.reshape(batch * seq, ATT_WIDTH)


def _conv_kernel(u_ref, halo_ref, w_ref, b_ref, lg_ref, lb_ref, o_ref, ext_ref, conv_ref, *, ts, rc, rn):
    C = u_ref.shape[-1]
    halo = halo_ref[...].astype(F32)
    ext_ref[0:CONV_HALO, :] = jnp.where(pl.program_id(1) == 0, 0.0, halo)
    ext_ref[CONV_HALO:CONV_HALO + ts, :] = u_ref[...].astype(F32)
    ext_ref[CONV_HALO + ts:, :] = jnp.zeros((SUBLANES, C), F32)
    first = CONV_HALO - (CONV_WIDTH - 1)

    def conv_chunk(r, _):
        r0 = pl.multiple_of(r * rc, rc)
        for c0 in range(0, C, LANES):
            win = ext_ref[pl.ds(r0, rc + CONV_HALO + SUBLANES), c0:c0 + LANES]
            acc = jnp.broadcast_to(b_ref[:, c0:c0 + LANES], (rc, LANES))
            for res in range(SUBLANES):
                part = None
                for a in range(CONV_HALO // SUBLANES + 1):
                    t = a * SUBLANES + res - first
                    if 0 <= t < CONV_WIDTH:
                        lo = a * SUBLANES
                        term = win[lo:lo + rc + SUBLANES, :] * w_ref[t:t + 1, c0:c0 + LANES]
                        part = term if part is None else part + term
                acc = acc + part[res:res + rc, :]
            conv_ref[pl.ds(r0, rc), c0:c0 + LANES] = acc
        return 0

    lax.fori_loop(0, ts // rc, conv_chunk, 0)

    for r0 in range(0, ts, rn):
        acc = conv_ref[r0:r0 + rn, :]
        mu = jnp.mean(acc, axis=-1, keepdims=True)
        d = acc - mu
        y = d * lax.rsqrt(jnp.mean(d * d, axis=-1, keepdims=True) + LN_EPS) * lg_ref[...] + lb_ref[...]
        o_ref[r0:r0 + rn, :] = (y * jax.nn.sigmoid(y)).astype(BF16)


def _conv(u, w, b, lg, lb, *, batch, seq, ts, rc=64, rn=32):
    C = u.shape[-1]
    u3 = u.reshape(batch, seq, C)
    per_tile = ts // CONV_HALO
    out = pl.pallas_call(
        functools.partial(_conv_kernel, ts=ts, rc=rc, rn=rn),
        grid=(batch, seq // ts),
        in_specs=[pl.BlockSpec((None, ts, C), lambda bb, i: (bb, i, 0)),
                  pl.BlockSpec((None, CONV_HALO, C), lambda bb, i: (bb, jnp.maximum(i * per_tile - 1, 0), 0)),
                  _resident((CONV_WIDTH, C)), _resident((1, C)), _resident((1, C)), _resident((1, C))],
        out_specs=pl.BlockSpec((None, ts, C), lambda bb, i: (bb, i, 0)),
        out_shape=jax.ShapeDtypeStruct((batch, seq, C), BF16),
        scratch_shapes=[pltpu.VMEM((ts + CONV_HALO + SUBLANES, C), F32), pltpu.VMEM((ts, C), F32)],
        compiler_params=_params("arbitrary", "arbitrary"),
        name="conv_ln_silu",
    )(u3, u3, w, b, lg, lb)
    return out.reshape(batch * seq, C)


def _out_proj_kernel(x_ref, att_ref, uc_ref, w_ref, h_ref, *, att_w):
    h = x_ref[...] + jnp.dot(att_ref[...], w_ref[:att_w, :], preferred_element_type=F32)
    h_ref[...] = h + jnp.dot(uc_ref[...], w_ref[att_w:, :], preferred_element_type=F32)


def _out_proj(x2, att, uc, w_out, *, tm):
    T, D = x2.shape
    row = lambda w: pl.BlockSpec((tm, w), lambda i: (i, 0))
    return pl.pallas_call(
        functools.partial(_out_proj_kernel, att_w=att.shape[1]),
        grid=(T // tm,),
        in_specs=[row(D), row(att.shape[1]), row(uc.shape[1]), _resident(w_out.shape)],
        out_specs=row(D),
        out_shape=jax.ShapeDtypeStruct((T, D), F32),
        compiler_params=_params("arbitrary"),
        name="out_proj",
    )(x2, att, uc, w_out)


def _ffn_kernel(h_ref, g2_ref, wg_ref, wu_ref, wd_ref, o_ref, hn_ref):
    f = pl.program_id(1)

    @pl.when(f == 0)
    def _():
        h = h_ref[...]
        hn_ref[...] = (_rms(h) * g2_ref[...]).astype(BF16)
        o_ref[...] = h

    hn = hn_ref[...]
    a = jnp.dot(hn, wg_ref[...], preferred_element_type=F32)
    b = jnp.dot(hn, wu_ref[...], preferred_element_type=F32)
    g = (a * jax.nn.sigmoid(a) * b).astype(BF16)
    o_ref[...] += jnp.dot(g, wd_ref[...], preferred_element_type=F32)


def _ffn(h, g2, wg, wu, wd, *, tm, tf):
    T, D = h.shape
    F = wg.shape[1]
    return pl.pallas_call(
        _ffn_kernel,
        grid=(T // tm, F // tf),
        in_specs=[pl.BlockSpec((tm, D), lambda i, f: (i, 0)), _resident((1, D)),
                  pl.BlockSpec((D, tf), lambda i, f: (0, f)), pl.BlockSpec((D, tf), lambda i, f: (0, f)),
                  pl.BlockSpec((tf, D), lambda i, f: (f, 0))],
        out_specs=pl.BlockSpec((tm, D), lambda i, f: (i, 0)),
        out_shape=jax.ShapeDtypeStruct((T, D), F32),
        scratch_shapes=[pltpu.VMEM((tm, D), BF16)],
        compiler_params=_params("arbitrary", "arbitrary"),
        name="swiglu_ffn",
    )(h, g2, wg, wu, wd)


def _tile(n, want):
    t = want
    while n % t:
        t //= 2
    return t


def _rope_tables(seq):
    inv_freq = ROPE_THETA ** (-jnp.arange(0, HEAD_DIM, 2, dtype=F32) / HEAD_DIM)
    ang = jnp.arange(seq, dtype=F32)[:, None] * inv_freq[None, :]
    ang = jnp.concatenate([ang, ang], axis=-1)
    sign = jnp.where(jnp.arange(HEAD_DIM) < HEAD_DIM // 2, -1.0, 1.0).astype(F32)
    return jnp.cos(ang), jnp.sin(ang) * sign


def kernel(x, norm1_g, w_in, q_norm_g, k_norm_g, lambda_q1, lambda_k1, lambda_q2, lambda_k2, subln_g,
           conv_w, conv_b, conv_ln_g, conv_ln_b, w_out, norm2_g, w_gate, w_up, w_down):
    B, S, D = x.shape
    T = B * S
    F = w_gate.shape[-1]
    cos, sin = _rope_tables(S)
    row = lambda a: a.reshape(1, -1).astype(F32)
    h = x.reshape(T, D)
    for l in range(w_in.shape[0]):
        lam_init = 0.8 - 0.6 * math.exp(-0.3 * l)
        q, k, v, u = _in_proj(h, row(norm1_g[l]), w_in[l].astype(BF16), row(q_norm_g[l]), row(k_norm_g[l]),
                              cos, sin, seq=S, tm=_tile(S, 512))
        lams = [row(a[l]) for a in (lambda_q1, lambda_k1, lambda_q2, lambda_k2)]
        score_bound = 1.02 * math.sqrt(HEAD_DIM) * jnp.max(jnp.abs(q_norm_g[l])) * jnp.max(jnp.abs(k_norm_g[l]))
        bounded = (score_bound <= SAFE_SCORE).astype(jnp.int32).reshape(1)
        att = _attention(bounded, lams, row(subln_g[l]), q, k, v, batch=B, seq=S, tq=_tile(S, 1024),
                         lam_init=lam_init)
        uc = _conv(u, conv_w[l].reshape(CONV_WIDTH, -1).astype(F32), row(conv_b[l]), row(conv_ln_g[l]),
                   row(conv_ln_b[l]), batch=B, seq=S, ts=_tile(S, 256))
        h = _out_proj(h, att, uc, w_out[l].astype(BF16), tm=_tile(T, 512))
        h = _ffn(h, row(norm2_g[l]), w_gate[l].astype(BF16), w_up[l].astype(BF16), w_down[l].astype(BF16),
                 tm=_tile(T, 1024), tf=_tile(F, 512))
    return h.reshape(B, S, D)
```

```python
import functools
import math

import jax
import jax.numpy as jnp
from jax import lax
from jax.experimental import pallas as pl
from jax.experimental.pallas import tpu as pltpu

ATT_HEADS = 4
HEAD_DIM = 128
V_DIM = 2 * HEAD_DIM
ATT_WIDTH = ATT_HEADS * V_DIM
CHUNK = 64
CONV_WIDTH = 31
CONV_HALO = 32
ROPE_THETA = 10000.0
EPS = 1e-6
LN_EPS = 1e-5
LOG2_E = 1.4426950408889634
SAFE_SCORE = 30.0

SUBLANES = 8
LANES = 128
VMEM_LIMIT_BYTES = 58 * 1024 * 1024

F32 = jnp.float32
BF16 = jnp.bfloat16


def _params(*semantics):
    return pltpu.CompilerParams(dimension_semantics=semantics, vmem_limit_bytes=VMEM_LIMIT_BYTES)


def _resident(shape):
    return pl.BlockSpec(shape, lambda *_: (0,) * len(shape), pipeline_mode=pl.Buffered(1))


def _rms(t, eps=EPS):
    return t * lax.rsqrt(jnp.mean(t * t, axis=-1, keepdims=True) + eps)


def _conv_ln_silu_units(ext_ref, conv_ref, w_ref, b_ref, lg_ref, lb_ref, o_ref, *, rc, rn):
    first = CONV_HALO - (CONV_WIDTH - 1)

    def conv_unit(r0, c0):
        acc = jnp.broadcast_to(b_ref[:, c0:c0 + LANES], (rc, LANES))
        for res in range(SUBLANES):
            part = None
            for a in range(CONV_HALO // SUBLANES + 1):
                t = a * SUBLANES + res - first
                if 0 <= t < CONV_WIDTH:
                    lo = r0 + a * SUBLANES
                    term = ext_ref[lo:lo + rc + SUBLANES, c0:c0 + LANES] * w_ref[t:t + 1, c0:c0 + LANES]
                    part = term if part is None else part + term
            acc = acc + part[res:res + rc, :]
        conv_ref[r0:r0 + rc, c0:c0 + LANES] = acc

    def norm_unit(r0):
        acc = conv_ref[r0:r0 + rn, :]
        mu = jnp.mean(acc, axis=-1, keepdims=True)
        d = acc - mu
        y = d * lax.rsqrt(jnp.mean(d * d, axis=-1, keepdims=True) + LN_EPS) * lg_ref[...] + lb_ref[...]
        o_ref[r0:r0 + rn, :] = (y * jax.nn.sigmoid(y)).astype(BF16)

    return conv_unit, norm_unit


def _in_proj_kernel(x_ref, g1_ref, w_ref, qg_ref, kg_ref, cos_ref, sin_ref, cw_ref, cb_ref, lg_ref, lb_ref,
                    q_ref, k_ref, v_ref, uc_ref, ext_ref, conv_ref, *, qk_cols, att_w, conv_ch, tm, n_pos):
    xn = (_rms(x_ref[...]) * g1_ref[...]).astype(BF16)
    cos = cos_ref[...]
    sin = sin_ref[...]

    first_tile = pl.program_id(0) % n_pos == 0

    @pl.when(first_tile)
    def _():
        ext_ref[0:CONV_HALO, :] = jnp.zeros((CONV_HALO, conv_ch), F32)

    @pl.when(jnp.logical_not(first_tile))
    def _():
        ext_ref[0:CONV_HALO, :] = ext_ref[tm:tm + CONV_HALO, :]

    ext_ref[CONV_HALO + tm:, :] = jnp.zeros((SUBLANES, conv_ch), F32)

    def proj(c0):
        return jnp.dot(xn, w_ref[:, c0:c0 + V_DIM], preferred_element_type=F32)

    g_off = 2 * qk_cols + att_w
    for j in range(conv_ch // V_DIM):
        ga = proj(g_off + j * V_DIM)
        gg = proj(g_off + conv_ch + j * V_DIM)
        ext_ref[CONV_HALO:CONV_HALO + tm, j * V_DIM:(j + 1) * V_DIM] = ga * jax.nn.sigmoid(gg)

    rc, rn = 64, 32
    conv_unit, norm_unit = _conv_ln_silu_units(ext_ref, conv_ref, cw_ref, cb_ref, lg_ref, lb_ref, uc_ref,
                                               rc=rc, rn=rn)
    for c0 in range(0, conv_ch, LANES):
        for r0 in range(0, tm, rc):
            conv_unit(r0, c0)
    for r0 in range(0, tm, rn):
        norm_unit(r0)

    def norm_rope(t, g, scale):
        y = _rms(t) * g
        y = y * cos + pltpu.roll(y, HEAD_DIM // 2, 1) * sin
        return (y * scale).astype(BF16)

    for o_ref, g_ref, base, scale in ((q_ref, qg_ref, 0, LOG2_E / math.sqrt(HEAD_DIM)),
                                      (k_ref, kg_ref, qk_cols, 1.0)):
        for h in range(qk_cols // V_DIM):
            t = proj(base + h * V_DIM)
            for c in range(2):
                lo = h * V_DIM + c * HEAD_DIM
                o_ref[:, lo:lo + HEAD_DIM] = norm_rope(t[:, c * HEAD_DIM:(c + 1) * HEAD_DIM], g_ref[...], scale)
    for h in range(att_w // V_DIM):
        v_ref[:, h * V_DIM:(h + 1) * V_DIM] = proj(2 * qk_cols + h * V_DIM).astype(BF16)


def _in_proj(x2, g1, w_in, qg, kg, cos, sin, conv_w, conv_b, ln_g, ln_b, *, seq, tm):
    T, D = x2.shape
    qk_cols = ATT_HEADS * 2 * HEAD_DIM
    conv_ch = (w_in.shape[1] - 2 * qk_cols - ATT_WIDTH) // 2
    n_pos = seq // tm
    row = lambda w: pl.BlockSpec((tm, w), lambda i: (i, 0))
    pos = pl.BlockSpec((tm, HEAD_DIM), lambda i: (i % n_pos, 0))
    return pl.pallas_call(
        functools.partial(_in_proj_kernel, qk_cols=qk_cols, att_w=ATT_WIDTH, conv_ch=conv_ch, tm=tm, n_pos=n_pos),
        grid=(T // tm,),
        in_specs=[row(D), _resident((1, D)), _resident(w_in.shape),
                  _resident((1, HEAD_DIM)), _resident((1, HEAD_DIM)), pos, pos, _resident((CONV_WIDTH, conv_ch)),
                  _resident((1, conv_ch)), _resident((1, conv_ch)), _resident((1, conv_ch))],
        out_specs=[row(qk_cols), row(qk_cols), row(ATT_WIDTH), row(conv_ch)],
        out_shape=[jax.ShapeDtypeStruct((T, qk_cols), BF16), jax.ShapeDtypeStruct((T, qk_cols), BF16),
                   jax.ShapeDtypeStruct((T, ATT_WIDTH), BF16), jax.ShapeDtypeStruct((T, conv_ch), BF16)],
        scratch_shapes=[pltpu.VMEM((CONV_HALO + tm + SUBLANES, conv_ch), F32), pltpu.VMEM((tm, conv_ch), F32)],
        compiler_params=_params("arbitrary"),
        name="in_proj_conv",
    )(x2, g1, w_in, qg, kg, cos, sin, conv_w, conv_b, ln_g, ln_b)


def _attn_kernel(bounded_ref, lq1_ref, lk1_ref, lq2_ref, lk2_ref, sg_ref, q_ref, k_ref, v_ref, o_ref,
                 acc1_ref, acc2_ref, *, tq, lam_init):
    i = pl.program_id(2)
    hq = tq // 2
    nt = (((1,), (1,)), ((), ()))
    accs = (acc1_ref, acc2_ref)

    def block(ref, j):
        return ref[pl.ds(pl.multiple_of(j * tq, tq), tq), :]

    def qk(qb, kb):
        return (lax.dot_general(qb[:, :HEAD_DIM], kb[:, :HEAD_DIM], nt, preferred_element_type=F32),
                lax.dot_general(qb[:, HEAD_DIM:], kb[:, HEAD_DIM:], nt, preferred_element_type=F32))

    def pv(p, vb):
        return jnp.dot(p.astype(BF16), vb, preferred_element_type=F32)

    def rowsum(p):
        return jnp.sum(p, axis=-1, keepdims=True)

    def visible(n_rows, n_cols, row0):
        rows = (lax.broadcasted_iota(jnp.int32, (n_rows, n_cols), 0) + row0) // CHUNK
        cols = lax.broadcasted_iota(jnp.int32, (n_rows, n_cols), 1) // CHUNK
        return cols <= rows

    def finish(l1, l2):
        lam = (jnp.exp(jnp.sum(lq1_ref[...] * lk1_ref[...], keepdims=True))
               - jnp.exp(jnp.sum(lq2_ref[...] * lk2_ref[...], keepdims=True)) + lam_init)
        o = acc1_ref[...] * (1.0 / l1) - lam * (acc2_ref[...] * (1.0 / l2))
        o_ref[...] = (_rms(o) * sg_ref[...] * (1.0 - lam_init)).astype(BF16)

    @pl.when(bounded_ref[0] != 0)
    def _():
        kd, vd = block(k_ref, i), block(v_ref, i)
        top = qk(q_ref[:hq, :], kd[:hq, :])
        bot = qk(q_ref[hq:, :], kd)
        vis_top, vis_bot = visible(hq, hq, 0), visible(hq, tq, hq)
        init = []
        for s_top, s_bot, acc_ref in zip(top, bot, accs):
            p_top = jnp.where(vis_top, jnp.exp2(s_top), 0.0)
            p_bot = jnp.where(vis_bot, jnp.exp2(s_bot), 0.0)
            acc_ref[:hq, :] = pv(p_top, vd[:hq, :])
            acc_ref[hq:, :] = pv(p_bot, vd)
            init.append(jnp.concatenate([rowsum(p_top), rowsum(p_bot)], axis=0))

        def body(j, ls):
            out = []
            for s, acc_ref, l in zip(qk(q_ref[...], block(k_ref, j)), accs, ls):
                p = jnp.exp2(s)
                acc_ref[...] += pv(p, block(v_ref, j))
                out.append(l + rowsum(p))
            return tuple(out)

        finish(*lax.fori_loop(0, i, body, tuple(init)))

    @pl.when(bounded_ref[0] == 0)
    def _():
        q = q_ref[...]
        vis = visible(tq, tq, 0)

        def scores(j):
            return qk(q, block(k_ref, j))

        init = []
        for s, acc_ref in zip(scores(i), accs):
            s = jnp.where(vis, s, -jnp.inf)
            m = jnp.max(s, axis=-1, keepdims=True)
            p = jnp.exp2(s - m)
            acc_ref[...] = pv(p, block(v_ref, i))
            init += [m, rowsum(p)]

        def body(j, carry):
            out = []
            for s, acc_ref, m, l in zip(scores(j), accs, carry[0::2], carry[1::2]):
                m_new = jnp.maximum(m, jnp.max(s, axis=-1, keepdims=True))
                a = jnp.exp2(m - m_new)
                p = jnp.exp2(s - m_new)
                acc_ref[...] = a * acc_ref[...] + pv(p, block(v_ref, j))
                out += [m_new, a * l + rowsum(p)]
            return tuple(out)

        _, l1, _, l2 = lax.fori_loop(0, i, body, tuple(init))
        finish(l1, l2)


def _attention(bounded, lams, sg, q, k, v, *, batch, seq, tq, lam_init):
    q3, k3, v3 = (a.reshape(batch, seq, a.shape[-1]) for a in (q, k, v))
    blk = pl.BlockSpec((None, tq, V_DIM), lambda b, h, i: (b, i, h))
    whole = pl.BlockSpec((None, seq, V_DIM), lambda b, h, i: (b, 0, h))
    out = pl.pallas_call(
        functools.partial(_attn_kernel, tq=tq, lam_init=lam_init),
        grid=(batch, ATT_HEADS, seq // tq),
        in_specs=[pl.BlockSpec(memory_space=pltpu.SMEM)] + [_resident((1, HEAD_DIM))] * 4
                 + [_resident((1, V_DIM)), blk, whole, whole],
        out_specs=blk,
        out_shape=jax.ShapeDtypeStruct((batch, seq, ATT_WIDTH), BF16),
        scratch_shapes=[pltpu.VMEM((tq, V_DIM), F32), pltpu.VMEM((tq, V_DIM), F32)],
        compiler_params=_params("arbitrary", "arbitrary", "arbitrary"),
        name="diff_attn",
    )(bounded, *lams, sg, q3, k3, v3)
    return out.reshape(batch * seq, ATT_WIDTH)


def _out_proj_kernel(x_ref, att_ref, uc_ref, w_ref, h_ref, *, att_w):
    h = x_ref[...] + jnp.dot(att_ref[...], w_ref[:att_w, :], preferred_element_type=F32)
    h_ref[...] = h + jnp.dot(uc_ref[...], w_ref[att_w:, :], preferred_element_type=F32)


def _out_proj(x2, att, uc, w_out, *, tm):
    T, D = x2.shape
    row = lambda w: pl.BlockSpec((tm, w), lambda i: (i, 0))
    return pl.pallas_call(
        functools.partial(_out_proj_kernel, att_w=att.shape[1]),
        grid=(T // tm,),
        in_specs=[row(D), row(att.shape[1]), row(uc.shape[1]), _resident(w_out.shape)],
        out_specs=row(D),
        out_shape=jax.ShapeDtypeStruct((T, D), F32),
        compiler_params=_params("arbitrary"),
        name="out_proj",
    )(x2, att, uc, w_out)


def _ffn_kernel(h_ref, g2_ref, wg_ref, wu_ref, wd_ref, o_ref, hn_ref):
    f = pl.program_id(1)

    @pl.when(f == 0)
    def _():
        h = h_ref[...]
        hn_ref[...] = (_rms(h) * g2_ref[...]).astype(BF16)
        o_ref[...] = h

    hn = hn_ref[...]
    a = jnp.dot(hn, wg_ref[...], preferred_element_type=F32)
    b = jnp.dot(hn, wu_ref[...], preferred_element_type=F32)
    g = (a * jax.nn.sigmoid(a) * b).astype(BF16)
    o_ref[...] += jnp.dot(g, wd_ref[...], preferred_element_type=F32)


def _ffn(h, g2, wg, wu, wd, *, tm, tf):
    T, D = h.shape
    F = wg.shape[1]
    return pl.pallas_call(
        _ffn_kernel,
        grid=(T // tm, F // tf),
        in_specs=[pl.BlockSpec((tm, D), lambda i, f: (i, 0)), _resident((1, D)),
                  pl.BlockSpec((D, tf), lambda i, f: (0, f)), pl.BlockSpec((D, tf), lambda i, f: (0, f)),
                  pl.BlockSpec((tf, D), lambda i, f: (f, 0))],
        out_specs=pl.BlockSpec((tm, D), lambda i, f: (i, 0)),
        out_shape=jax.ShapeDtypeStruct((T, D), F32),
        scratch_shapes=[pltpu.VMEM((tm, D), BF16)],
        compiler_params=_params("arbitrary", "arbitrary"),
        name="swiglu_ffn",
    )(h, g2, wg, wu, wd)


def _tile(n, want):
    t = want
    while n % t:
        t //= 2
    return t


def _rope_tables(seq):
    inv_freq = ROPE_THETA ** (-jnp.arange(0, HEAD_DIM, 2, dtype=F32) / HEAD_DIM)
    ang = jnp.arange(seq, dtype=F32)[:, None] * inv_freq[None, :]
    ang = jnp.concatenate([ang, ang], axis=-1)
    sign = jnp.where(jnp.arange(HEAD_DIM) < HEAD_DIM // 2, -1.0, 1.0).astype(F32)
    return jnp.cos(ang), jnp.sin(ang) * sign


def kernel(x, norm1_g, w_in, q_norm_g, k_norm_g, lambda_q1, lambda_k1, lambda_q2, lambda_k2, subln_g,
           conv_w, conv_b, conv_ln_g, conv_ln_b, w_out, norm2_g, w_gate, w_up, w_down):
    B, S, D = x.shape
    T = B * S
    F = w_gate.shape[-1]
    cos, sin = _rope_tables(S)
    row = lambda a: a.reshape(1, -1).astype(F32)
    h = x.reshape(T, D)
    for l in range(w_in.shape[0]):
        lam_init = 0.8 - 0.6 * math.exp(-0.3 * l)
        q, k, v, uc = _in_proj(h, row(norm1_g[l]), w_in[l].astype(BF16), row(q_norm_g[l]), row(k_norm_g[l]),
                               cos, sin, conv_w[l].reshape(CONV_WIDTH, -1).astype(F32), row(conv_b[l]),
                               row(conv_ln_g[l]), row(conv_ln_b[l]), seq=S, tm=_tile(S, 512))
        lams = [row(a[l]) for a in (lambda_q1, lambda_k1, lambda_q2, lambda_k2)]
        score_bound = 1.02 * math.sqrt(HEAD_DIM) * jnp.max(jnp.abs(q_norm_g[l])) * jnp.max(jnp.abs(k_norm_g[l]))
        bounded = (score_bound <= SAFE_SCORE).astype(jnp.int32).reshape(1)
        att = _attention(bounded, lams, row(subln_g[l]), q, k, v, batch=B, seq=S, tq=_tile(S, 1024),
                         lam_init=lam_init)
        h = _out_proj(h, att, uc, w_out[l].astype(BF16), tm=_tile(T, 512))
        h = _ffn(h, row(norm2_g[l]), w_gate[l].astype(BF16), w_up[l].astype(BF16), w_down[l].astype(BF16),
                 tm=_tile(T, 1024), tf=_tile(F, 512))
    return h.reshape(B, S, D)
```

```python
import functools
import math

import jax
import jax.numpy as jnp
from jax import lax
from jax.experimental import pallas as pl
from jax.experimental.pallas import tpu as pltpu

ATT_HEADS = 4
HEAD_DIM = 128
V_DIM = 2 * HEAD_DIM
ATT_WIDTH = ATT_HEADS * V_DIM
CHUNK = 64
CONV_WIDTH = 31
CONV_HALO = 32
ROPE_THETA = 10000.0
EPS = 1e-6
LN_EPS = 1e-5
LOG2_E = 1.4426950408889634
SAFE_SCORE = 30.0

SUBLANES = 8
LANES = 128
VMEM_LIMIT_BYTES = 58 * 1024 * 1024

F32 = jnp.float32
BF16 = jnp.bfloat16


def _params(*semantics):
    return pltpu.CompilerParams(dimension_semantics=semantics, vmem_limit_bytes=VMEM_LIMIT_BYTES)


def _resident(shape):
    return pl.BlockSpec(shape, lambda *_: (0,) * len(shape), pipeline_mode=pl.Buffered(1))


def _rms(t, eps=EPS):
    return t * lax.rsqrt(jnp.mean(t * t, axis=-1, keepdims=True) + eps)


def _conv_ln_silu_units(ext_ref, conv_ref, w_ref, b_ref, lg_ref, lb_ref, o_ref, *, rc, rn):
    first = CONV_HALO - (CONV_WIDTH - 1)

    def conv_unit(r0, c0):
        acc = jnp.broadcast_to(b_ref[:, c0:c0 + LANES], (rc, LANES))
        for res in range(SUBLANES):
            part = None
            for a in range(CONV_HALO // SUBLANES + 1):
                t = a * SUBLANES + res - first
                if 0 <= t < CONV_WIDTH:
                    lo = r0 + a * SUBLANES
                    term = ext_ref[lo:lo + rc + SUBLANES, c0:c0 + LANES] * w_ref[t:t + 1, c0:c0 + LANES]
                    part = term if part is None else part + term
            acc = acc + part[res:res + rc, :]
        conv_ref[r0:r0 + rc, c0:c0 + LANES] = acc

    def norm_unit(r0):
        acc = conv_ref[r0:r0 + rn, :]
        mu = jnp.mean(acc, axis=-1, keepdims=True)
        d = acc - mu
        y = d * lax.rsqrt(jnp.mean(d * d, axis=-1, keepdims=True) + LN_EPS) * lg_ref[...] + lb_ref[...]
        o_ref[r0:r0 + rn, :] = (y * jax.nn.sigmoid(y)).astype(BF16)

    return conv_unit, norm_unit


def _in_proj_kernel(x_ref, g1_ref, w_ref, qg_ref, kg_ref, cos_ref, sin_ref, cw_ref, cb_ref, lg_ref, lb_ref,
                    q_ref, k_ref, v_ref, uc_ref, ext_ref, conv_ref, *, qk_cols, att_w, conv_ch, tm, n_pos):
    xn = (_rms(x_ref[...]) * g1_ref[...]).astype(BF16)
    cos = cos_ref[...]
    sin = sin_ref[...]

    first_tile = pl.program_id(0) % n_pos == 0

    @pl.when(first_tile)
    def _():
        ext_ref[0:CONV_HALO, :] = jnp.zeros((CONV_HALO, conv_ch), F32)

    @pl.when(jnp.logical_not(first_tile))
    def _():
        ext_ref[0:CONV_HALO, :] = ext_ref[tm:tm + CONV_HALO, :]

    ext_ref[CONV_HALO + tm:, :] = jnp.zeros((SUBLANES, conv_ch), F32)

    def proj(c0):
        return jnp.dot(xn, w_ref[:, c0:c0 + V_DIM], preferred_element_type=F32)

    g_off = 2 * qk_cols + att_w
    for j in range(conv_ch // V_DIM):
        ga = proj(g_off + j * V_DIM)
        gg = proj(g_off + conv_ch + j * V_DIM)
        ext_ref[CONV_HALO:CONV_HALO + tm, j * V_DIM:(j + 1) * V_DIM] = ga * jax.nn.sigmoid(gg)

    rc, rn = min(tm, 128), 32
    conv_unit, norm_unit = _conv_ln_silu_units(ext_ref, conv_ref, cw_ref, cb_ref, lg_ref, lb_ref, uc_ref,
                                               rc=rc, rn=rn)
    for c0 in range(0, conv_ch, LANES):
        for r0 in range(0, tm, rc):
            conv_unit(r0, c0)
    for r0 in range(0, tm, rn):
        norm_unit(r0)

    def norm_rope(t, g, scale):
        y = _rms(t) * g
        y = y * cos + pltpu.roll(y, HEAD_DIM // 2, 1) * sin
        return (y * scale).astype(BF16)

    for o_ref, g_ref, base, scale in ((q_ref, qg_ref, 0, LOG2_E / math.sqrt(HEAD_DIM)),
                                      (k_ref, kg_ref, qk_cols, 1.0)):
        for h in range(qk_cols // V_DIM):
            t = proj(base + h * V_DIM)
            for c in range(2):
                lo = h * V_DIM + c * HEAD_DIM
                o_ref[:, lo:lo + HEAD_DIM] = norm_rope(t[:, c * HEAD_DIM:(c + 1) * HEAD_DIM], g_ref[...], scale)
    for h in range(att_w // V_DIM):
        v_ref[:, h * V_DIM:(h + 1) * V_DIM] = proj(2 * qk_cols + h * V_DIM).astype(BF16)


def _in_proj(x2, g1, w_in, qg, kg, cos, sin, conv_w, conv_b, ln_g, ln_b, *, seq, tm):
    T, D = x2.shape
    qk_cols = ATT_HEADS * 2 * HEAD_DIM
    conv_ch = (w_in.shape[1] - 2 * qk_cols - ATT_WIDTH) // 2
    n_pos = seq // tm
    row = lambda w: pl.BlockSpec((tm, w), lambda i: (i, 0))
    pos = pl.BlockSpec((tm, HEAD_DIM), lambda i: (i % n_pos, 0))
    return pl.pallas_call(
        functools.partial(_in_proj_kernel, qk_cols=qk_cols, att_w=ATT_WIDTH, conv_ch=conv_ch, tm=tm, n_pos=n_pos),
        grid=(T // tm,),
        in_specs=[row(D), _resident((1, D)), _resident(w_in.shape),
                  _resident((1, HEAD_DIM)), _resident((1, HEAD_DIM)), pos, pos, _resident((CONV_WIDTH, conv_ch)),
                  _resident((1, conv_ch)), _resident((1, conv_ch)), _resident((1, conv_ch))],
        out_specs=[row(qk_cols), row(qk_cols), row(ATT_WIDTH), row(conv_ch)],
        out_shape=[jax.ShapeDtypeStruct((T, qk_cols), BF16), jax.ShapeDtypeStruct((T, qk_cols), BF16),
                   jax.ShapeDtypeStruct((T, ATT_WIDTH), BF16), jax.ShapeDtypeStruct((T, conv_ch), BF16)],
        scratch_shapes=[pltpu.VMEM((CONV_HALO + tm + SUBLANES, conv_ch), F32), pltpu.VMEM((tm, conv_ch), F32)],
        compiler_params=_params("arbitrary"),
        name="in_proj_conv",
    )(x2, g1, w_in, qg, kg, cos, sin, conv_w, conv_b, ln_g, ln_b)


def _attn_kernel(bounded_ref, lq1_ref, lk1_ref, lq2_ref, lk2_ref, sg_ref, q_ref, k_ref, v_ref, o_ref,
                 acc1_ref, acc2_ref, *, tq, lam_init):
    i = pl.program_id(2)
    hq = tq // 2
    nt = (((1,), (1,)), ((), ()))
    accs = (acc1_ref, acc2_ref)

    def block(ref, j):
        return ref[pl.ds(pl.multiple_of(j * tq, tq), tq), :]

    def qk(qb, kb):
        return (lax.dot_general(qb[:, :HEAD_DIM], kb[:, :HEAD_DIM], nt, preferred_element_type=F32),
                lax.dot_general(qb[:, HEAD_DIM:], kb[:, HEAD_DIM:], nt, preferred_element_type=F32))

    def pv(p, vb):
        return jnp.dot(p.astype(BF16), vb, preferred_element_type=F32)

    def rowsum(p):
        return jnp.sum(p, axis=-1, keepdims=True)

    def visible(n_rows, n_cols, row0):
        rows = (lax.broadcasted_iota(jnp.int32, (n_rows, n_cols), 0) + row0) // CHUNK
        cols = lax.broadcasted_iota(jnp.int32, (n_rows, n_cols), 1) // CHUNK
        return cols <= rows

    def finish(l1, l2):
        lam = (jnp.exp(jnp.sum(lq1_ref[...] * lk1_ref[...], keepdims=True))
               - jnp.exp(jnp.sum(lq2_ref[...] * lk2_ref[...], keepdims=True)) + lam_init)
        o = acc1_ref[...] * (1.0 / l1) - lam * (acc2_ref[...] * (1.0 / l2))
        o_ref[...] = (_rms(o) * sg_ref[...] * (1.0 - lam_init)).astype(BF16)

    @pl.when(bounded_ref[0] != 0)
    def _():
        kd, vd = block(k_ref, i), block(v_ref, i)
        top = qk(q_ref[:hq, :], kd[:hq, :])
        bot = qk(q_ref[hq:, :], kd)
        vis_top, vis_bot = visible(hq, hq, 0), visible(hq, tq, hq)
        init = []
        for s_top, s_bot, acc_ref in zip(top, bot, accs):
            p_top = jnp.where(vis_top, jnp.exp2(s_top), 0.0)
            p_bot = jnp.where(vis_bot, jnp.exp2(s_bot), 0.0)
            acc_ref[:hq, :] = pv(p_top, vd[:hq, :])
            acc_ref[hq:, :] = pv(p_bot, vd)
            init.append(jnp.concatenate([rowsum(p_top), rowsum(p_bot)], axis=0))

        def body(j, ls):
            out = []
            for s, acc_ref, l in zip(qk(q_ref[...], block(k_ref, j)), accs, ls):
                p = jnp.exp2(s)
                acc_ref[...] += pv(p, block(v_ref, j))
                out.append(l + rowsum(p))
            return tuple(out)

        ls = lax.fori_loop(0, lax.shift_right_logical(i, 1), lambda jj, ls: body(2 * jj + 1, body(2 * jj, ls)),
                           tuple(init))
        finish(*lax.fori_loop(i - lax.bitwise_and(i, 1), i, body, ls))

    @pl.when(bounded_ref[0] == 0)
    def _():
        q = q_ref[...]
        vis = visible(tq, tq, 0)

        def scores(j):
            return qk(q, block(k_ref, j))

        init = []
        for s, acc_ref in zip(scores(i), accs):
            s = jnp.where(vis, s, -jnp.inf)
            m = jnp.max(s, axis=-1, keepdims=True)
            p = jnp.exp2(s - m)
            acc_ref[...] = pv(p, block(v_ref, i))
            init += [m, rowsum(p)]

        def body(j, carry):
            out = []
            for s, acc_ref, m, l in zip(scores(j), accs, carry[0::2], carry[1::2]):
                m_new = jnp.maximum(m, jnp.max(s, axis=-1, keepdims=True))
                a = jnp.exp2(m - m_new)
                p = jnp.exp2(s - m_new)
                acc_ref[...] = a * acc_ref[...] + pv(p, block(v_ref, j))
                out += [m_new, a * l + rowsum(p)]
            return tuple(out)

        _, l1, _, l2 = lax.fori_loop(0, i, body, tuple(init))
        finish(l1, l2)


def _attention(bounded, lams, sg, q, k, v, *, batch, seq, tq, lam_init):
    q3, k3, v3 = (a.reshape(batch, seq, a.shape[-1]) for a in (q, k, v))
    blk = pl.BlockSpec((None, tq, V_DIM), lambda b, h, i: (b, i, h))
    whole = pl.BlockSpec((None, seq, V_DIM), lambda b, h, i: (b, 0, h))
    out = pl.pallas_call(
        functools.partial(_attn_kernel, tq=tq, lam_init=lam_init),
        grid=(batch, ATT_HEADS, seq // tq),
        in_specs=[pl.BlockSpec(memory_space=pltpu.SMEM)] + [_resident((1, HEAD_DIM))] * 4
                 + [_resident((1, V_DIM)), blk, whole, whole],
        out_specs=blk,
        out_shape=jax.ShapeDtypeStruct((batch, seq, ATT_WIDTH), BF16),
        scratch_shapes=[pltpu.VMEM((tq, V_DIM), F32), pltpu.VMEM((tq, V_DIM), F32)],
        compiler_params=_params("arbitrary", "arbitrary", "arbitrary"),
        name="diff_attn",
    )(bounded, *lams, sg, q3, k3, v3)
    return out.reshape(batch * seq, ATT_WIDTH)


def _out_proj_kernel(x_ref, att_ref, uc_ref, w_ref, h_ref, *, att_w):
    h = x_ref[...] + jnp.dot(att_ref[...], w_ref[:att_w, :], preferred_element_type=F32)
    h_ref[...] = h + jnp.dot(uc_ref[...], w_ref[att_w:, :], preferred_element_type=F32)


def _out_proj(x2, att, uc, w_out, *, tm):
    T, D = x2.shape
    row = lambda w: pl.BlockSpec((tm, w), lambda i: (i, 0))
    return pl.pallas_call(
        functools.partial(_out_proj_kernel, att_w=att.shape[1]),
        grid=(T // tm,),
        in_specs=[row(D), row(att.shape[1]), row(uc.shape[1]), _resident(w_out.shape)],
        out_specs=row(D),
        out_shape=jax.ShapeDtypeStruct((T, D), F32),
        compiler_params=_params("arbitrary"),
        name="out_proj",
    )(x2, att, uc, w_out)


def _ffn_kernel(h_ref, g2_ref, wg_ref, wu_ref, wd_ref, o_ref, hn_ref):
    f = pl.program_id(1)

    @pl.when(f == 0)
    def _():
        h = h_ref[...]
        hn_ref[...] = (_rms(h) * g2_ref[...]).astype(BF16)
        o_ref[...] = h

    hn = hn_ref[...]
    a = jnp.dot(hn, wg_ref[...], preferred_element_type=F32)
    b = jnp.dot(hn, wu_ref[...], preferred_element_type=F32)
    g = (a * jax.nn.sigmoid(a) * b).astype(BF16)
    o_ref[...] += jnp.dot(g, wd_ref[...], preferred_element_type=F32)


def _ffn(h, g2, wg, wu, wd, *, tm, tf):
    T, D = h.shape
    F = wg.shape[1]
    return pl.pallas_call(
        _ffn_kernel,
        grid=(T // tm, F // tf),
        in_specs=[pl.BlockSpec((tm, D), lambda i, f: (i, 0)), _resident((1, D)),
                  pl.BlockSpec((D, tf), lambda i, f: (0, f)), pl.BlockSpec((D, tf), lambda i, f: (0, f)),
                  pl.BlockSpec((tf, D), lambda i, f: (f, 0))],
        out_specs=pl.BlockSpec((tm, D), lambda i, f: (i, 0)),
        out_shape=jax.ShapeDtypeStruct((T, D), F32),
        scratch_shapes=[pltpu.VMEM((tm, D), BF16)],
        compiler_params=_params("arbitrary", "arbitrary"),
        name="swiglu_ffn",
    )(h, g2, wg, wu, wd)


def _tile(n, want):
    t = want
    while n % t:
        t //= 2
    return t


def _rope_tables(seq):
    inv_freq = ROPE_THETA ** (-jnp.arange(0, HEAD_DIM, 2, dtype=F32) / HEAD_DIM)
    ang = jnp.arange(seq, dtype=F32)[:, None] * inv_freq[None, :]
    ang = jnp.concatenate([ang, ang], axis=-1)
    sign = jnp.where(jnp.arange(HEAD_DIM) < HEAD_DIM // 2, -1.0, 1.0).astype(F32)
    return jnp.cos(ang), jnp.sin(ang) * sign


def kernel(x, norm1_g, w_in, q_norm_g, k_norm_g, lambda_q1, lambda_k1, lambda_q2, lambda_k2, subln_g,
           conv_w, conv_b, conv_ln_g, conv_ln_b, w_out, norm2_g, w_gate, w_up, w_down):
    B, S, D = x.shape
    T = B * S
    F = w_gate.shape[-1]
    cos, sin = _rope_tables(S)
    row = lambda a: a.reshape(1, -1).astype(F32)
    h = x.reshape(T, D)
    for l in range(w_in.shape[0]):
        lam_init = 0.8 - 0.6 * math.exp(-0.3 * l)
        q, k, v, uc = _in_proj(h, row(norm1_g[l]), w_in[l].astype(BF16), row(q_norm_g[l]), row(k_norm_g[l]),
                               cos, sin, conv_w[l].reshape(CONV_WIDTH, -1).astype(F32), row(conv_b[l]),
                               row(conv_ln_g[l]), row(conv_ln_b[l]), seq=S, tm=_tile(S, 512))
        lams = [row(a[l]) for a in (lambda_q1, lambda_k1, lambda_q2, lambda_k2)]
        score_bound = 1.02 * math.sqrt(HEAD_DIM) * jnp.max(jnp.abs(q_norm_g[l])) * jnp.max(jnp.abs(k_norm_g[l]))
        bounded = (score_bound <= SAFE_SCORE).astype(jnp.int32).reshape(1)
        att = _attention(bounded, lams, row(subln_g[l]), q, k, v, batch=B, seq=S, tq=_tile(S, 1024),
                         lam_init=lam_init)
        h = _out_proj(h, att, uc, w_out[l].astype(BF16), tm=_tile(T, 512))
        h = _ffn(h, row(norm2_g[l]), w_gate[l].astype(BF16), w_up[l].astype(BF16), w_down[l].astype(BF16),
                 tm=_tile(T, 1024), tf=_tile(F, 512))
    return h.reshape(B, S, D)
```

```python
import functools
import math

import jax
import jax.numpy as jnp
from jax import lax
from jax.experimental import pallas as pl
from jax.experimental.pallas import tpu as pltpu

ATT_HEADS = 4
HEAD_DIM = 128
V_DIM = 2 * HEAD_DIM
ATT_WIDTH = ATT_HEADS * V_DIM
CHUNK = 64
CONV_WIDTH = 31
CONV_HALO = 32
ROPE_THETA = 10000.0
EPS = 1e-6
LN_EPS = 1e-5
LOG2_E = 1.4426950408889634
SAFE_SCORE = 30.0

SUBLANES = 8
LANES = 128
VMEM_LIMIT_BYTES = 58 * 1024 * 1024

F32 = jnp.float32
BF16 = jnp.bfloat16


def _params(*semantics):
    return pltpu.CompilerParams(dimension_semantics=semantics, vmem_limit_bytes=VMEM_LIMIT_BYTES)


def _resident(shape):
    return pl.BlockSpec(shape, lambda *_: (0,) * len(shape), pipeline_mode=pl.Buffered(1))


def _rms(t, eps=EPS):
    return t * lax.rsqrt(jnp.mean(t * t, axis=-1, keepdims=True) + eps)


def _conv_ln_silu_units(ext_ref, conv_ref, w_ref, b_ref, lg_ref, lb_ref, o_ref, *, rc, rn):
    first = CONV_HALO - (CONV_WIDTH - 1)

    def conv_unit(r0, c0, after=None):
        bias = b_ref[:, c0:c0 + LANES]
        if after is not None:
            bias = bias + after
        acc = jnp.broadcast_to(bias, (rc, LANES))
        for res in range(SUBLANES):
            part = None
            for a in range(CONV_HALO // SUBLANES + 1):
                t = a * SUBLANES + res - first
                if 0 <= t < CONV_WIDTH:
                    lo = r0 + a * SUBLANES
                    term = ext_ref[lo:lo + rc + SUBLANES, c0:c0 + LANES] * w_ref[t:t + 1, c0:c0 + LANES]
                    part = term if part is None else part + term
            acc = acc + part[res:res + rc, :]
        conv_ref[r0:r0 + rc, c0:c0 + LANES] = acc
        return acc

    def norm_unit(r0):
        acc = conv_ref[r0:r0 + rn, :]
        mu = jnp.mean(acc, axis=-1, keepdims=True)
        d = acc - mu
        y = d * lax.rsqrt(jnp.mean(d * d, axis=-1, keepdims=True) + LN_EPS) * lg_ref[...] + lb_ref[...]
        o_ref[r0:r0 + rn, :] = (y * jax.nn.sigmoid(y)).astype(BF16)

    return conv_unit, norm_unit


def _in_proj_kernel(x_ref, g1_ref, w_ref, qg_ref, kg_ref, cos_ref, sin_ref, cw_ref, cb_ref, lg_ref, lb_ref,
                    q_ref, k_ref, v_ref, uc_ref, ext_ref, conv_ref, *, qk_cols, att_w, conv_ch, tm, n_pos):
    xn = (_rms(x_ref[...]) * g1_ref[...]).astype(BF16)
    cos = cos_ref[...]
    sin = sin_ref[...]

    first_tile = pl.program_id(0) % n_pos == 0

    @pl.when(first_tile)
    def _():
        ext_ref[0:CONV_HALO, :] = jnp.zeros((CONV_HALO, conv_ch), F32)

    @pl.when(jnp.logical_not(first_tile))
    def _():
        ext_ref[0:CONV_HALO, :] = ext_ref[tm:tm + CONV_HALO, :]

    ext_ref[CONV_HALO + tm:, :] = jnp.zeros((SUBLANES, conv_ch), F32)

    def proj(c0):
        return jnp.dot(xn, w_ref[:, c0:c0 + V_DIM], preferred_element_type=F32)

    g_off = 2 * qk_cols + att_w

    def glu(j):
        ga = proj(g_off + j * V_DIM)
        gg = proj(g_off + conv_ch + j * V_DIM)
        ext_ref[CONV_HALO:CONV_HALO + tm, j * V_DIM:(j + 1) * V_DIM] = ga * jax.nn.sigmoid(gg)

    def zero_after(t):
        bits = lax.bitcast_convert_type(t[0:1, 0:LANES], jnp.uint32)
        half = jnp.uint32(16)
        bits = lax.shift_right_logical(lax.shift_right_logical(bits, half), half)
        return lax.bitcast_convert_type(bits, F32)

    def norm_rope(t, g, scale):
        y = _rms(t) * g
        y = y * cos + pltpu.roll(y, HEAD_DIM // 2, 1) * sin
        return (y * scale).astype(BF16)

    def qk_head(o_ref, g_ref, base, scale, h):
        t = proj(base + h * V_DIM)
        for c in range(2):
            lo = h * V_DIM + c * HEAD_DIM
            o_ref[:, lo:lo + HEAD_DIM] = norm_rope(t[:, c * HEAD_DIM:(c + 1) * HEAD_DIM], g_ref[...], scale)
        return zero_after(t)

    def v_head(h):
        t = proj(2 * qk_cols + h * V_DIM)
        v_ref[:, h * V_DIM:(h + 1) * V_DIM] = t.astype(BF16)
        return zero_after(t)

    heads = range(qk_cols // V_DIM)
    matmuls = ([functools.partial(qk_head, q_ref, qg_ref, 0, LOG2_E / math.sqrt(HEAD_DIM), h) for h in heads]
               + [functools.partial(qk_head, k_ref, kg_ref, qk_cols, 1.0, h) for h in heads]
               + [functools.partial(v_head, h) for h in range(att_w // V_DIM)])
    rc, rn = min(tm, 128), 32
    conv_unit, norm_unit = _conv_ln_silu_units(ext_ref, conv_ref, cw_ref, cb_ref, lg_ref, lb_ref, uc_ref,
                                               rc=rc, rn=rn)
    lane_groups = conv_ch // LANES
    free = 2
    chain = None
    for c in range(lane_groups):
        if c % (V_DIM // LANES) == 0:
            glu(c // (V_DIM // LANES))
        after = chain
        if c >= free:
            after = matmuls[c - free]() + (0.0 if chain is None else chain)
        for r0 in range(0, tm, rc):
            acc = conv_unit(r0, c * LANES, after)
        chain = zero_after(acc)
    for task in matmuls[lane_groups - free:]:
        task()
    for r0 in range(0, tm, rn):
        norm_unit(r0)


def _in_proj(x2, g1, w_in, qg, kg, cos, sin, conv_w, conv_b, ln_g, ln_b, *, seq, tm):
    T, D = x2.shape
    qk_cols = ATT_HEADS * 2 * HEAD_DIM
    conv_ch = (w_in.shape[1] - 2 * qk_cols - ATT_WIDTH) // 2
    n_pos = seq // tm
    row = lambda w: pl.BlockSpec((tm, w), lambda i: (i, 0))
    pos = pl.BlockSpec((tm, HEAD_DIM), lambda i: (i % n_pos, 0))
    return pl.pallas_call(
        functools.partial(_in_proj_kernel, qk_cols=qk_cols, att_w=ATT_WIDTH, conv_ch=conv_ch, tm=tm, n_pos=n_pos),
        grid=(T // tm,),
        in_specs=[row(D), _resident((1, D)), _resident(w_in.shape),
                  _resident((1, HEAD_DIM)), _resident((1, HEAD_DIM)), pos, pos, _resident((CONV_WIDTH, conv_ch)),
                  _resident((1, conv_ch)), _resident((1, conv_ch)), _resident((1, conv_ch))],
        out_specs=[row(qk_cols), row(qk_cols), row(ATT_WIDTH), row(conv_ch)],
        out_shape=[jax.ShapeDtypeStruct((T, qk_cols), BF16), jax.ShapeDtypeStruct((T, qk_cols), BF16),
                   jax.ShapeDtypeStruct((T, ATT_WIDTH), BF16), jax.ShapeDtypeStruct((T, conv_ch), BF16)],
        scratch_shapes=[pltpu.VMEM((CONV_HALO + tm + SUBLANES, conv_ch), F32), pltpu.VMEM((tm, conv_ch), F32)],
        compiler_params=_params("arbitrary"),
        name="in_proj_conv",
    )(x2, g1, w_in, qg, kg, cos, sin, conv_w, conv_b, ln_g, ln_b)


def _attn_kernel(bounded_ref, lq1_ref, lk1_ref, lq2_ref, lk2_ref, sg_ref, q_ref, k_ref, v_ref, o_ref,
                 acc1_ref, acc2_ref, *, tq, lam_init):
    i = pl.program_id(2)
    hq = tq // 2
    nt = (((1,), (1,)), ((), ()))
    accs = (acc1_ref, acc2_ref)

    def block(ref, j):
        return ref[pl.ds(pl.multiple_of(j * tq, tq), tq), :]

    def qk(qb, kb):
        return (lax.dot_general(qb[:, :HEAD_DIM], kb[:, :HEAD_DIM], nt, preferred_element_type=F32),
                lax.dot_general(qb[:, HEAD_DIM:], kb[:, HEAD_DIM:], nt, preferred_element_type=F32))

    def pv(p, vb):
        return jnp.dot(p.astype(BF16), vb, preferred_element_type=F32)

    def rowsum(p):
        return jnp.sum(p, axis=-1, keepdims=True)

    def visible(n_rows, n_cols, row0):
        rows = (lax.broadcasted_iota(jnp.int32, (n_rows, n_cols), 0) + row0) // CHUNK
        cols = lax.broadcasted_iota(jnp.int32, (n_rows, n_cols), 1) // CHUNK
        return cols <= rows

    def finish(l1, l2):
        lam = (jnp.exp(jnp.sum(lq1_ref[...] * lk1_ref[...], keepdims=True))
               - jnp.exp(jnp.sum(lq2_ref[...] * lk2_ref[...], keepdims=True)) + lam_init)
        o = acc1_ref[...] * (1.0 / l1) - lam * (acc2_ref[...] * (1.0 / l2))
        o_ref[...] = (_rms(o) * sg_ref[...] * (1.0 - lam_init)).astype(BF16)

    @pl.when(bounded_ref[0] != 0)
    def _():
        kd, vd = block(k_ref, i), block(v_ref, i)
        top = qk(q_ref[:hq, :], kd[:hq, :])
        bot = qk(q_ref[hq:, :], kd)
        vis_top, vis_bot = visible(hq, hq, 0), visible(hq, tq, hq)
        init = []
        for s_top, s_bot, acc_ref in zip(top, bot, accs):
            p_top = jnp.where(vis_top, jnp.exp2(s_top), 0.0)
            p_bot = jnp.where(vis_bot, jnp.exp2(s_bot), 0.0)
            acc_ref[:hq, :] = pv(p_top, vd[:hq, :])
            acc_ref[hq:, :] = pv(p_bot, vd)
            init.append(jnp.concatenate([rowsum(p_top), rowsum(p_bot)], axis=0))

        def body(j, ls):
            out = []
            for s, acc_ref, l in zip(qk(q_ref[...], block(k_ref, j)), accs, ls):
                p = jnp.exp2(s)
                acc_ref[...] += pv(p, block(v_ref, j))
                out.append(l + rowsum(p))
            return tuple(out)

        ls = lax.fori_loop(0, lax.shift_right_logical(i, 1), lambda jj, ls: body(2 * jj + 1, body(2 * jj, ls)),
                           tuple(init))
        finish(*lax.fori_loop(i - lax.bitwise_and(i, 1), i, body, ls))

    @pl.when(bounded_ref[0] == 0)
    def _():
        q = q_ref[...]
        vis = visible(tq, tq, 0)

        def scores(j):
            return qk(q, block(k_ref, j))

        init = []
        for s, acc_ref in zip(scores(i), accs):
            s = jnp.where(vis, s, -jnp.inf)
            m = jnp.max(s, axis=-1, keepdims=True)
            p = jnp.exp2(s - m)
            acc_ref[...] = pv(p, block(v_ref, i))
            init += [m, rowsum(p)]

        def body(j, carry):
            out = []
            for s, acc_ref, m, l in zip(scores(j), accs, carry[0::2], carry[1::2]):
                m_new = jnp.maximum(m, jnp.max(s, axis=-1, keepdims=True))
                a = jnp.exp2(m - m_new)
                p = jnp.exp2(s - m_new)
                acc_ref[...] = a * acc_ref[...] + pv(p, block(v_ref, j))
                out += [m_new, a * l + rowsum(p)]
            return tuple(out)

        _, l1, _, l2 = lax.fori_loop(0, i, body, tuple(init))
        finish(l1, l2)


def _attention(bounded, lams, sg, q, k, v, *, batch, seq, tq, lam_init):
    q3, k3, v3 = (a.reshape(batch, seq, a.shape[-1]) for a in (q, k, v))
    blk = pl.BlockSpec((None, tq, V_DIM), lambda b, h, i: (b, i, h))
    whole = pl.BlockSpec((None, seq, V_DIM), lambda b, h, i: (b, 0, h))
    out = pl.pallas_call(
        functools.partial(_attn_kernel, tq=tq, lam_init=lam_init),
        grid=(batch, ATT_HEADS, seq // tq),
        in_specs=[pl.BlockSpec(memory_space=pltpu.SMEM)] + [_resident((1, HEAD_DIM))] * 4
                 + [_resident((1, V_DIM)), blk, whole, whole],
        out_specs=blk,
        out_shape=jax.ShapeDtypeStruct((batch, seq, ATT_WIDTH), BF16),
        scratch_shapes=[pltpu.VMEM((tq, V_DIM), F32), pltpu.VMEM((tq, V_DIM), F32)],
        compiler_params=_params("arbitrary", "arbitrary", "arbitrary"),
        name="diff_attn",
    )(bounded, *lams, sg, q3, k3, v3)
    return out.reshape(batch * seq, ATT_WIDTH)


def _out_proj_kernel(x_ref, att_ref, uc_ref, w_ref, h_ref, *, att_w):
    h = x_ref[...] + jnp.dot(att_ref[...], w_ref[:att_w, :], preferred_element_type=F32)
    h_ref[...] = h + jnp.dot(uc_ref[...], w_ref[att_w:, :], preferred_element_type=F32)


def _out_proj(x2, att, uc, w_out, *, tm):
    T, D = x2.shape
    row = lambda w: pl.BlockSpec((tm, w), lambda i: (i, 0))
    return pl.pallas_call(
        functools.partial(_out_proj_kernel, att_w=att.shape[1]),
        grid=(T // tm,),
        in_specs=[row(D), row(att.shape[1]), row(uc.shape[1]), _resident(w_out.shape)],
        out_specs=row(D),
        out_shape=jax.ShapeDtypeStruct((T, D), F32),
        compiler_params=_params("arbitrary"),
        name="out_proj",
    )(x2, att, uc, w_out)


def _ffn_kernel(h_ref, g2_ref, wg_ref, wu_ref, wd_ref, o_ref, hn_ref):
    f = pl.program_id(1)

    @pl.when(f == 0)
    def _():
        h = h_ref[...]
        hn_ref[...] = (_rms(h) * g2_ref[...]).astype(BF16)
        o_ref[...] = h

    hn = hn_ref[...]
    a = jnp.dot(hn, wg_ref[...], preferred_element_type=F32)
    b = jnp.dot(hn, wu_ref[...], preferred_element_type=F32)
    g = (a * jax.nn.sigmoid(a) * b).astype(BF16)
    o_ref[...] += jnp.dot(g, wd_ref[...], preferred_element_type=F32)


def _ffn(h, g2, wg, wu, wd, *, tm, tf):
    T, D = h.shape
    F = wg.shape[1]
    return pl.pallas_call(
        _ffn_kernel,
        grid=(T // tm, F // tf),
        in_specs=[pl.BlockSpec((tm, D), lambda i, f: (i, 0)), _resident((1, D)),
                  pl.BlockSpec((D, tf), lambda i, f: (0, f)), pl.BlockSpec((D, tf), lambda i, f: (0, f)),
                  pl.BlockSpec((tf, D), lambda i, f: (f, 0))],
        out_specs=pl.BlockSpec((tm, D), lambda i, f: (i, 0)),
        out_shape=jax.ShapeDtypeStruct((T, D), F32),
        scratch_shapes=[pltpu.VMEM((tm, D), BF16)],
        compiler_params=_params("arbitrary", "arbitrary"),
        name="swiglu_ffn",
    )(h, g2, wg, wu, wd)


def _tile(n, want):
    t = want
    while n % t:
        t //= 2
    return t


def _rope_tables(seq):
    inv_freq = ROPE_THETA ** (-jnp.arange(0, HEAD_DIM, 2, dtype=F32) / HEAD_DIM)
    ang = jnp.arange(seq, dtype=F32)[:, None] * inv_freq[None, :]
    ang = jnp.concatenate([ang, ang], axis=-1)
    sign = jnp.where(jnp.arange(HEAD_DIM) < HEAD_DIM // 2, -1.0, 1.0).astype(F32)
    return jnp.cos(ang), jnp.sin(ang) * sign


def kernel(x, norm1_g, w_in, q_norm_g, k_norm_g, lambda_q1, lambda_k1, lambda_q2, lambda_k2, subln_g,
           conv_w, conv_b, conv_ln_g, conv_ln_b, w_out, norm2_g, w_gate, w_up, w_down):
    B, S, D = x.shape
    T = B * S
    F = w_gate.shape[-1]
    cos, sin = _rope_tables(S)
    row = lambda a: a.reshape(1, -1).astype(F32)
    h = x.reshape(T, D)
    for l in range(w_in.shape[0]):
        lam_init = 0.8 - 0.6 * math.exp(-0.3 * l)
        q, k, v, uc = _in_proj(h, row(norm1_g[l]), w_in[l].astype(BF16), row(q_norm_g[l]), row(k_norm_g[l]),
                               cos, sin, conv_w[l].reshape(CONV_WIDTH, -1).astype(F32), row(conv_b[l]),
                               row(conv_ln_g[l]), row(conv_ln_b[l]), seq=S, tm=_tile(S, 512))
        lams = [row(a[l]) for a in (lambda_q1, lambda_k1, lambda_q2, lambda_k2)]
        score_bound = 1.02 * math.sqrt(HEAD_DIM) * jnp.max(jnp.abs(q_norm_g[l])) * jnp.max(jnp.abs(k_norm_g[l]))
        bounded = (score_bound <= SAFE_SCORE).astype(jnp.int32).reshape(1)
        att = _attention(bounded, lams, row(subln_g[l]), q, k, v, batch=B, seq=S, tq=_tile(S, 1024),
                         lam_init=lam_init)
        h = _out_proj(h, att, uc, w_out[l].astype(BF16), tm=_tile(T, 512))
        h = _ffn(h, row(norm2_g[l]), w_gate[l].astype(BF16), w_up[l].astype(BF16), w_down[l].astype(BF16),
                 tm=_tile(T, 1024), tf=_tile(F, 512))
    return h.reshape(B, S, D)
```

```python
import functools
import math

import jax
import jax.numpy as jnp
from jax import lax
from jax.experimental import pallas as pl
from jax.experimental.pallas import tpu as pltpu

ATT_HEADS = 4
HEAD_DIM = 128
V_DIM = 2 * HEAD_DIM
ATT_WIDTH = ATT_HEADS * V_DIM
CHUNK = 64
CONV_WIDTH = 31
CONV_HALO = 32
ROPE_THETA = 10000.0
EPS = 1e-6
LN_EPS = 1e-5
LOG2_E = 1.4426950408889634
SAFE_SCORE = 30.0

SUBLANES = 8
BF16_ROWS = 16
LANES = 128
VMEM_LIMIT_BYTES = 58 * 1024 * 1024

F32 = jnp.float32
BF16 = jnp.bfloat16


def _params(*semantics):
    return pltpu.CompilerParams(dimension_semantics=semantics, vmem_limit_bytes=VMEM_LIMIT_BYTES)


def _resident(shape):
    return pl.BlockSpec(shape, lambda *_: (0,) * len(shape), pipeline_mode=pl.Buffered(1))


def _rms(t, eps=EPS):
    return t * lax.rsqrt(jnp.mean(t * t, axis=-1, keepdims=True) + eps)


def _conv_ln_silu_units(ext_ref, conv_ref, w_ref, b_ref, lg_ref, lb_ref, o_ref, *, rc, rn):
    first = CONV_HALO - (CONV_WIDTH - 1)

    def conv_unit(r0, c0, after=None):
        bias = b_ref[:, c0:c0 + LANES]
        if after is not None:
            bias = bias + after
        acc = jnp.broadcast_to(bias, (rc, LANES))
        for res in range(SUBLANES):
            part = None
            for a in range(CONV_HALO // SUBLANES + 1):
                t = a * SUBLANES + res - first
                if 0 <= t < CONV_WIDTH:
                    lo = r0 + a * SUBLANES
                    term = ext_ref[lo:lo + rc + SUBLANES, c0:c0 + LANES] * w_ref[t:t + 1, c0:c0 + LANES]
                    part = term if part is None else part + term
            acc = acc + part[res:res + rc, :]
        conv_ref[r0:r0 + rc, c0:c0 + LANES] = acc
        return acc

    def norm_unit(r0):
        acc = conv_ref[r0:r0 + rn, :]
        mu = jnp.mean(acc, axis=-1, keepdims=True)
        d = acc - mu
        y = d * lax.rsqrt(jnp.mean(d * d, axis=-1, keepdims=True) + LN_EPS) * lg_ref[...] + lb_ref[...]
        o_ref[r0:r0 + rn, :] = (y * jax.nn.sigmoid(y)).astype(BF16)

    return conv_unit, norm_unit


def _in_proj_kernel(x_ref, g1_ref, w_ref, qg_ref, kg_ref, cos_ref, sin_ref, cw_ref, cb_ref, lg_ref, lb_ref,
                    q_ref, k_ref, v_ref, uc_ref, ext_ref, conv_ref, *, qk_cols, att_w, conv_ch, tm, n_pos):
    xn = (_rms(x_ref[...]) * g1_ref[...]).astype(BF16)
    cos = cos_ref[...]
    sin = sin_ref[...]

    first_tile = pl.program_id(0) % n_pos == 0

    @pl.when(first_tile)
    def _():
        ext_ref[0:CONV_HALO, :] = jnp.zeros((CONV_HALO, conv_ch), F32)

    @pl.when(jnp.logical_not(first_tile))
    def _():
        ext_ref[0:CONV_HALO, :] = ext_ref[tm:tm + CONV_HALO, :]

    ext_ref[CONV_HALO + tm:, :] = jnp.zeros((SUBLANES, conv_ch), F32)

    def proj(c0):
        return jnp.dot(xn, w_ref[:, c0:c0 + V_DIM], preferred_element_type=F32)

    g_off = 2 * qk_cols + att_w

    def glu(j):
        ga = proj(g_off + j * V_DIM)
        gg = proj(g_off + conv_ch + j * V_DIM)
        ext_ref[CONV_HALO:CONV_HALO + tm, j * V_DIM:(j + 1) * V_DIM] = ga * jax.nn.sigmoid(gg)

    def zero_after(t):
        bits = lax.bitcast_convert_type(t[0:1, 0:LANES], jnp.uint32)
        half = jnp.uint32(16)
        bits = lax.shift_right_logical(lax.shift_right_logical(bits, half), half)
        return lax.bitcast_convert_type(bits, F32)

    def norm_rope(t, g, scale):
        y = _rms(t) * g
        y = y * cos + pltpu.roll(y, HEAD_DIM // 2, 1) * sin
        return (y * scale).astype(BF16)

    def qk_head(o_ref, g_ref, base, scale, h):
        t = proj(base + h * V_DIM)
        for c in range(2):
            lo = h * V_DIM + c * HEAD_DIM
            o_ref[:, lo:lo + HEAD_DIM] = norm_rope(t[:, c * HEAD_DIM:(c + 1) * HEAD_DIM], g_ref[...], scale)
        return zero_after(t)

    def v_head(h):
        t = proj(2 * qk_cols + h * V_DIM)
        v_ref[:, h * V_DIM:(h + 1) * V_DIM] = t.astype(BF16)
        return zero_after(t)

    heads = range(qk_cols // V_DIM)
    matmuls = ([functools.partial(qk_head, q_ref, qg_ref, 0, LOG2_E / math.sqrt(HEAD_DIM), h) for h in heads]
               + [functools.partial(qk_head, k_ref, kg_ref, qk_cols, 1.0, h) for h in heads]
               + [functools.partial(v_head, h) for h in range(att_w // V_DIM)])
    rc, rn = min(tm, 128), 32
    conv_unit, norm_unit = _conv_ln_silu_units(ext_ref, conv_ref, cw_ref, cb_ref, lg_ref, lb_ref, uc_ref,
                                               rc=rc, rn=rn)
    lane_groups = conv_ch // LANES
    free = 2
    chain = None
    for c in range(lane_groups):
        if c % (V_DIM // LANES) == 0:
            glu(c // (V_DIM // LANES))
        after = chain
        if c >= free:
            after = matmuls[c - free]() + (0.0 if chain is None else chain)
        for r0 in range(0, tm, rc):
            acc = conv_unit(r0, c * LANES, after)
        chain = zero_after(acc)
    for task in matmuls[lane_groups - free:]:
        task()
    for r0 in range(0, tm, rn):
        norm_unit(r0)


def _in_proj(x2, g1, w_in, qg, kg, cos, sin, conv_w, conv_b, ln_g, ln_b, *, seq, tm):
    T, D = x2.shape
    qk_cols = ATT_HEADS * 2 * HEAD_DIM
    conv_ch = (w_in.shape[1] - 2 * qk_cols - ATT_WIDTH) // 2
    n_pos = seq // tm
    row = lambda w: pl.BlockSpec((tm, w), lambda i: (i, 0))
    pos = pl.BlockSpec((tm, HEAD_DIM), lambda i: (i % n_pos, 0))
    return pl.pallas_call(
        functools.partial(_in_proj_kernel, qk_cols=qk_cols, att_w=ATT_WIDTH, conv_ch=conv_ch, tm=tm, n_pos=n_pos),
        grid=(T // tm,),
        in_specs=[row(D), _resident((1, D)), _resident(w_in.shape),
                  _resident((1, HEAD_DIM)), _resident((1, HEAD_DIM)), pos, pos, _resident((CONV_WIDTH, conv_ch)),
                  _resident((1, conv_ch)), _resident((1, conv_ch)), _resident((1, conv_ch))],
        out_specs=[row(qk_cols), row(qk_cols), row(ATT_WIDTH), row(conv_ch)],
        out_shape=[jax.ShapeDtypeStruct((T, qk_cols), BF16), jax.ShapeDtypeStruct((T, qk_cols), BF16),
                   jax.ShapeDtypeStruct((T, ATT_WIDTH), BF16), jax.ShapeDtypeStruct((T, conv_ch), BF16)],
        scratch_shapes=[pltpu.VMEM((CONV_HALO + tm + SUBLANES, conv_ch), F32), pltpu.VMEM((tm, conv_ch), F32)],
        compiler_params=_params("arbitrary"),
        name="in_proj_conv",
    )(x2, g1, w_in, qg, kg, cos, sin, conv_w, conv_b, ln_g, ln_b)


def _attn_kernel(bounded_ref, lq1_ref, lk1_ref, lq2_ref, lk2_ref, sg_ref, q_ref, k_ref, v_ref, *refs,
                 tq, lam_init, n_cast):
    cast_in, o_ref, cast_out = refs[:n_cast], refs[n_cast], refs[n_cast + 1:2 * n_cast + 1]
    acc1_ref, acc2_ref = refs[2 * n_cast + 1:]

    def cast_weight_slabs():
        for src_ref, dst_ref in zip(cast_in, cast_out):
            dst_ref[...] = src_ref[...].astype(BF16)

    i = pl.program_id(2)
    hq = tq // 2
    nt = (((1,), (1,)), ((), ()))
    accs = (acc1_ref, acc2_ref)

    def block(ref, j):
        return ref[pl.ds(pl.multiple_of(j * tq, tq), tq), :]

    def qk(qb, kb):
        return (lax.dot_general(qb[:, :HEAD_DIM], kb[:, :HEAD_DIM], nt, preferred_element_type=F32),
                lax.dot_general(qb[:, HEAD_DIM:], kb[:, HEAD_DIM:], nt, preferred_element_type=F32))

    def pv(p, vb):
        return jnp.dot(p.astype(BF16), vb, preferred_element_type=F32)

    def rowsum(p):
        return jnp.sum(p, axis=-1, keepdims=True)

    def visible(n_rows, n_cols, row0):
        rows = (lax.broadcasted_iota(jnp.int32, (n_rows, n_cols), 0) + row0) // CHUNK
        cols = lax.broadcasted_iota(jnp.int32, (n_rows, n_cols), 1) // CHUNK
        return cols <= rows

    def finish(l1, l2):
        lam = (jnp.exp(jnp.sum(lq1_ref[...] * lk1_ref[...], keepdims=True))
               - jnp.exp(jnp.sum(lq2_ref[...] * lk2_ref[...], keepdims=True)) + lam_init)
        o = acc1_ref[...] * (1.0 / l1) - lam * (acc2_ref[...] * (1.0 / l2))
        o_ref[...] = (_rms(o) * sg_ref[...] * (1.0 - lam_init)).astype(BF16)

    @pl.when(bounded_ref[0] != 0)
    def _():
        kd, vd = block(k_ref, i), block(v_ref, i)
        top = qk(q_ref[:hq, :], kd[:hq, :])
        bot = qk(q_ref[hq:, :], kd)
        vis_top, vis_bot = visible(hq, hq, 0), visible(hq, tq, hq)
        cast_weight_slabs()
        init = []
        for s_top, s_bot, acc_ref in zip(top, bot, accs):
            p_top = jnp.where(vis_top, jnp.exp2(s_top), 0.0)
            p_bot = jnp.where(vis_bot, jnp.exp2(s_bot), 0.0)
            acc_ref[:hq, :] = pv(p_top, vd[:hq, :])
            acc_ref[hq:, :] = pv(p_bot, vd)
            init.append(jnp.concatenate([rowsum(p_top), rowsum(p_bot)], axis=0))

        def body(j, ls):
            out = []
            for s, acc_ref, l in zip(qk(q_ref[...], block(k_ref, j)), accs, ls):
                p = jnp.exp2(s)
                acc_ref[...] += pv(p, block(v_ref, j))
                out.append(l + rowsum(p))
            return tuple(out)

        ls = lax.fori_loop(0, lax.shift_right_logical(i, 1), lambda jj, ls: body(2 * jj + 1, body(2 * jj, ls)),
                           tuple(init))
        finish(*lax.fori_loop(i - lax.bitwise_and(i, 1), i, body, ls))

    @pl.when(bounded_ref[0] == 0)
    def _():
        q = q_ref[...]
        vis = visible(tq, tq, 0)
        cast_weight_slabs()

        def scores(j):
            return qk(q, block(k_ref, j))

        init = []
        for s, acc_ref in zip(scores(i), accs):
            s = jnp.where(vis, s, -jnp.inf)
            m = jnp.max(s, axis=-1, keepdims=True)
            p = jnp.exp2(s - m)
            acc_ref[...] = pv(p, block(v_ref, i))
            init += [m, rowsum(p)]

        def body(j, carry):
            out = []
            for s, acc_ref, m, l in zip(scores(j), accs, carry[0::2], carry[1::2]):
                m_new = jnp.maximum(m, jnp.max(s, axis=-1, keepdims=True))
                a = jnp.exp2(m - m_new)
                p = jnp.exp2(s - m_new)
                acc_ref[...] = a * acc_ref[...] + pv(p, block(v_ref, j))
                out += [m_new, a * l + rowsum(p)]
            return tuple(out)

        _, l1, _, l2 = lax.fori_loop(0, i, body, tuple(init))
        finish(l1, l2)


def _attention(bounded, lams, sg, q, k, v, weights, *, batch, seq, tq, lam_init):
    q3, k3, v3 = (a.reshape(batch, seq, a.shape[-1]) for a in (q, k, v))
    nq = seq // tq
    steps = batch * ATT_HEADS * nq
    blk = pl.BlockSpec((None, tq, V_DIM), lambda b, h, i: (b, i, h))
    whole = pl.BlockSpec((None, seq, V_DIM), lambda b, h, i: (b, 0, h))
    slabs = []
    for w in weights:
        group = 1
        while w.shape[0] % (steps // group) or (w.shape[0] // (steps // group)) % BF16_ROWS:
            group *= 2
        rows = w.shape[0] // (steps // group)
        slabs.append(pl.BlockSpec((rows, w.shape[1]),
                                  lambda b, h, i, g=group: (((b * ATT_HEADS + h) * nq + i) // g, 0)))
    outs = pl.pallas_call(
        functools.partial(_attn_kernel, tq=tq, lam_init=lam_init, n_cast=len(weights)),
        grid=(batch, ATT_HEADS, nq),
        in_specs=[pl.BlockSpec(memory_space=pltpu.SMEM)] + [_resident((1, HEAD_DIM))] * 4
                 + [_resident((1, V_DIM)), blk, whole, whole] + slabs,
        out_specs=[blk] + slabs,
        out_shape=[jax.ShapeDtypeStruct((batch, seq, ATT_WIDTH), BF16)]
                  + [jax.ShapeDtypeStruct(w.shape, BF16) for w in weights],
        scratch_shapes=[pltpu.VMEM((tq, V_DIM), F32), pltpu.VMEM((tq, V_DIM), F32)],
        compiler_params=_params("arbitrary", "arbitrary", "arbitrary"),
        name="diff_attn",
    )(bounded, *lams, sg, q3, k3, v3, *weights)
    return outs[0].reshape(batch * seq, ATT_WIDTH), outs[1:]


def _out_proj_kernel(x_ref, att_ref, uc_ref, w_ref, h_ref, *, att_w):
    h = x_ref[...] + jnp.dot(att_ref[...], w_ref[:att_w, :], preferred_element_type=F32)
    h_ref[...] = h + jnp.dot(uc_ref[...], w_ref[att_w:, :], preferred_element_type=F32)


def _out_proj(x2, att, uc, w_out, *, tm):
    T, D = x2.shape
    row = lambda w: pl.BlockSpec((tm, w), lambda i: (i, 0))
    return pl.pallas_call(
        functools.partial(_out_proj_kernel, att_w=att.shape[1]),
        grid=(T // tm,),
        in_specs=[row(D), row(att.shape[1]), row(uc.shape[1]), _resident(w_out.shape)],
        out_specs=row(D),
        out_shape=jax.ShapeDtypeStruct((T, D), F32),
        compiler_params=_params("arbitrary"),
        name="out_proj",
    )(x2, att, uc, w_out)


def _ffn_kernel(h_ref, g2_ref, wg_ref, wu_ref, wd_ref, o_ref, hn_ref):
    f = pl.program_id(1)

    @pl.when(f == 0)
    def _():
        h = h_ref[...]
        hn_ref[...] = (_rms(h) * g2_ref[...]).astype(BF16)
        o_ref[...] = h

    hn = hn_ref[...]
    a = jnp.dot(hn, wg_ref[...], preferred_element_type=F32)
    b = jnp.dot(hn, wu_ref[...], preferred_element_type=F32)
    g = (a * jax.nn.sigmoid(a) * b).astype(BF16)
    o_ref[...] += jnp.dot(g, wd_ref[...], preferred_element_type=F32)


def _ffn(h, g2, wg, wu, wd, *, tm, tf):
    T, D = h.shape
    F = wg.shape[1]
    return pl.pallas_call(
        _ffn_kernel,
        grid=(T // tm, F // tf),
        in_specs=[pl.BlockSpec((tm, D), lambda i, f: (i, 0)), _resident((1, D)),
                  pl.BlockSpec((D, tf), lambda i, f: (0, f)), pl.BlockSpec((D, tf), lambda i, f: (0, f)),
                  pl.BlockSpec((tf, D), lambda i, f: (f, 0))],
        out_specs=pl.BlockSpec((tm, D), lambda i, f: (i, 0)),
        out_shape=jax.ShapeDtypeStruct((T, D), F32),
        scratch_shapes=[pltpu.VMEM((tm, D), BF16)],
        compiler_params=_params("arbitrary", "arbitrary"),
        name="swiglu_ffn",
    )(h, g2, wg, wu, wd)


def _tile(n, want):
    t = want
    while n % t:
        t //= 2
    return t


def _rope_tables(seq):
    inv_freq = ROPE_THETA ** (-jnp.arange(0, HEAD_DIM, 2, dtype=F32) / HEAD_DIM)
    ang = jnp.arange(seq, dtype=F32)[:, None] * inv_freq[None, :]
    ang = jnp.concatenate([ang, ang], axis=-1)
    sign = jnp.where(jnp.arange(HEAD_DIM) < HEAD_DIM // 2, -1.0, 1.0).astype(F32)
    return jnp.cos(ang), jnp.sin(ang) * sign


def kernel(x, norm1_g, w_in, q_norm_g, k_norm_g, lambda_q1, lambda_k1, lambda_q2, lambda_k2, subln_g,
           conv_w, conv_b, conv_ln_g, conv_ln_b, w_out, norm2_g, w_gate, w_up, w_down):
    B, S, D = x.shape
    T = B * S
    F = w_gate.shape[-1]
    cos, sin = _rope_tables(S)
    row = lambda a: a.reshape(1, -1).astype(F32)
    h = x.reshape(T, D)
    for l in range(w_in.shape[0]):
        lam_init = 0.8 - 0.6 * math.exp(-0.3 * l)
        q, k, v, uc = _in_proj(h, row(norm1_g[l]), w_in[l].astype(BF16), row(q_norm_g[l]), row(k_norm_g[l]),
                               cos, sin, conv_w[l].reshape(CONV_WIDTH, -1).astype(F32), row(conv_b[l]),
                               row(conv_ln_g[l]), row(conv_ln_b[l]), seq=S, tm=_tile(S, 512))
        lams = [row(a[l]) for a in (lambda_q1, lambda_k1, lambda_q2, lambda_k2)]
        score_bound = 1.02 * math.sqrt(HEAD_DIM) * jnp.max(jnp.abs(q_norm_g[l])) * jnp.max(jnp.abs(k_norm_g[l]))
        bounded = (score_bound <= SAFE_SCORE).astype(jnp.int32).reshape(1)
        att, (wo, wg, wu, wd) = _attention(bounded, lams, row(subln_g[l]), q, k, v,
                                           [w_out[l], w_gate[l], w_up[l], w_down[l]],
                                           batch=B, seq=S, tq=_tile(S, 1024), lam_init=lam_init)
        h = _out_proj(h, att, uc, wo, tm=_tile(T, 512))
        h = _ffn(h, row(norm2_g[l]), wg, wu, wd, tm=_tile(T, 1024), tf=_tile(F, 512))
    return h.reshape(B, S, D)
```

```python
import functools
import math

import jax
import jax.numpy as jnp
from jax import lax
from jax.experimental import pallas as pl
from jax.experimental.pallas import tpu as pltpu

ATT_HEADS = 4
HEAD_DIM = 128
V_DIM = 2 * HEAD_DIM
ATT_WIDTH = ATT_HEADS * V_DIM
CHUNK = 64
CONV_WIDTH = 31
CONV_HALO = 32
ROPE_THETA = 10000.0
EPS = 1e-6
LN_EPS = 1e-5
LOG2_E = 1.4426950408889634
SAFE_SCORE = 30.0

SUBLANES = 8
BF16_ROWS = 16
LANES = 128
VMEM_LIMIT_BYTES = 58 * 1024 * 1024

F32 = jnp.float32
BF16 = jnp.bfloat16


def _params(*semantics):
    return pltpu.CompilerParams(dimension_semantics=semantics, vmem_limit_bytes=VMEM_LIMIT_BYTES)


def _resident(shape):
    return pl.BlockSpec(shape, lambda *_: (0,) * len(shape), pipeline_mode=pl.Buffered(1))


def _rms(t, eps=EPS):
    return t * lax.rsqrt(jnp.mean(t * t, axis=-1, keepdims=True) + eps)


def _conv_ln_silu_units(ext_ref, conv_ref, w_ref, b_ref, lg_ref, lb_ref, o_ref, *, rc, rn):
    first = CONV_HALO - (CONV_WIDTH - 1)

    def conv_unit(r0, c0, after=None):
        bias = b_ref[:, c0:c0 + LANES]
        if after is not None:
            bias = bias + after
        acc = jnp.broadcast_to(bias, (rc, LANES))
        for res in range(SUBLANES):
            part = None
            for a in range(CONV_HALO // SUBLANES + 1):
                t = a * SUBLANES + res - first
                if 0 <= t < CONV_WIDTH:
                    lo = r0 + a * SUBLANES
                    term = ext_ref[lo:lo + rc + SUBLANES, c0:c0 + LANES] * w_ref[t:t + 1, c0:c0 + LANES]
                    part = term if part is None else part + term
            acc = acc + part[res:res + rc, :]
        conv_ref[r0:r0 + rc, c0:c0 + LANES] = acc
        return acc

    def norm_unit(r0):
        acc = conv_ref[r0:r0 + rn, :]
        mu = jnp.mean(acc, axis=-1, keepdims=True)
        d = acc - mu
        y = d * lax.rsqrt(jnp.mean(d * d, axis=-1, keepdims=True) + LN_EPS) * lg_ref[...] + lb_ref[...]
        o_ref[r0:r0 + rn, :] = (y * jax.nn.sigmoid(y)).astype(BF16)

    return conv_unit, norm_unit


def _in_proj_kernel(x_ref, g1_ref, w_ref, qg_ref, kg_ref, cos_ref, sin_ref, cw_ref, cb_ref, lg_ref, lb_ref,
                    q_ref, k_ref, v_ref, uc_ref, ext_ref, conv_ref, *, qk_cols, att_w, conv_ch, tm, n_pos):
    xn = (_rms(x_ref[...]) * g1_ref[...]).astype(BF16)
    cos = cos_ref[...]
    sin = sin_ref[...]

    first_tile = pl.program_id(0) % n_pos == 0

    @pl.when(first_tile)
    def _():
        ext_ref[0:CONV_HALO, :] = jnp.zeros((CONV_HALO, conv_ch), F32)

    @pl.when(jnp.logical_not(first_tile))
    def _():
        ext_ref[0:CONV_HALO, :] = ext_ref[tm:tm + CONV_HALO, :]

    ext_ref[CONV_HALO + tm:, :] = jnp.zeros((SUBLANES, conv_ch), F32)

    def proj(c0):
        return jnp.dot(xn, w_ref[:, c0:c0 + V_DIM], preferred_element_type=F32)

    g_off = 2 * qk_cols + att_w

    def glu(j):
        ga = proj(g_off + j * V_DIM)
        gg = proj(g_off + conv_ch + j * V_DIM)
        ext_ref[CONV_HALO:CONV_HALO + tm, j * V_DIM:(j + 1) * V_DIM] = ga * jax.nn.sigmoid(gg)

    def zero_after(t):
        bits = lax.bitcast_convert_type(t[0:1, 0:LANES], jnp.uint32)
        half = jnp.uint32(16)
        bits = lax.shift_right_logical(lax.shift_right_logical(bits, half), half)
        return lax.bitcast_convert_type(bits, F32)

    def norm_rope(t, g, scale):
        y = _rms(t) * g
        y = y * cos + pltpu.roll(y, HEAD_DIM // 2, 1) * sin
        return y * scale

    def qk_head(o_ref, g_ref, base, scale, h):
        t = proj(base + h * V_DIM)
        for c in range(2):
            lo = h * V_DIM + c * HEAD_DIM
            y = norm_rope(t[:, c * HEAD_DIM:(c + 1) * HEAD_DIM], g_ref[...], scale)
            o_ref[:, lo:lo + HEAD_DIM] = y.astype(BF16)
        return zero_after(y[-SUBLANES:, :])

    def v_head(h):
        t = proj(2 * qk_cols + h * V_DIM)
        v_ref[:, h * V_DIM:(h + 1) * V_DIM] = t.astype(BF16)
        return zero_after(t[-SUBLANES:, -LANES:])

    heads = range(qk_cols // V_DIM)
    matmuls = ([functools.partial(qk_head, q_ref, qg_ref, 0, LOG2_E / math.sqrt(HEAD_DIM), h) for h in heads]
               + [functools.partial(qk_head, k_ref, kg_ref, qk_cols, 1.0, h) for h in heads]
               + [functools.partial(v_head, h) for h in range(att_w // V_DIM)])
    rc, rn = min(tm, 128), 32
    conv_unit, norm_unit = _conv_ln_silu_units(ext_ref, conv_ref, cw_ref, cb_ref, lg_ref, lb_ref, uc_ref,
                                               rc=rc, rn=rn)
    lane_groups = conv_ch // LANES
    free = 2
    chain = None
    for c in range(lane_groups):
        if c % (V_DIM // LANES) == 0:
            glu(c // (V_DIM // LANES))
        after = chain
        if c >= free:
            after = matmuls[c - free]() + (0.0 if chain is None else chain)
        for r0 in range(0, tm, rc):
            after = chain = zero_after(conv_unit(r0, c * LANES, after))
    for task in matmuls[lane_groups - free:]:
        task()
    for r0 in range(0, tm, rn):
        norm_unit(r0)


def _in_proj(x2, g1, w_in, qg, kg, cos, sin, conv_w, conv_b, ln_g, ln_b, *, seq, tm):
    T, D = x2.shape
    qk_cols = ATT_HEADS * 2 * HEAD_DIM
    conv_ch = (w_in.shape[1] - 2 * qk_cols - ATT_WIDTH) // 2
    n_pos = seq // tm
    row = lambda w: pl.BlockSpec((tm, w), lambda i: (i, 0))
    pos = pl.BlockSpec((tm, HEAD_DIM), lambda i: (i % n_pos, 0))
    return pl.pallas_call(
        functools.partial(_in_proj_kernel, qk_cols=qk_cols, att_w=ATT_WIDTH, conv_ch=conv_ch, tm=tm, n_pos=n_pos),
        grid=(T // tm,),
        in_specs=[row(D), _resident((1, D)), _resident(w_in.shape),
                  _resident((1, HEAD_DIM)), _resident((1, HEAD_DIM)), pos, pos, _resident((CONV_WIDTH, conv_ch)),
                  _resident((1, conv_ch)), _resident((1, conv_ch)), _resident((1, conv_ch))],
        out_specs=[row(qk_cols), row(qk_cols), row(ATT_WIDTH), row(conv_ch)],
        out_shape=[jax.ShapeDtypeStruct((T, qk_cols), BF16), jax.ShapeDtypeStruct((T, qk_cols), BF16),
                   jax.ShapeDtypeStruct((T, ATT_WIDTH), BF16), jax.ShapeDtypeStruct((T, conv_ch), BF16)],
        scratch_shapes=[pltpu.VMEM((CONV_HALO + tm + SUBLANES, conv_ch), F32), pltpu.VMEM((tm, conv_ch), F32)],
        compiler_params=_params("arbitrary"),
        name="in_proj_conv",
    )(x2, g1, w_in, qg, kg, cos, sin, conv_w, conv_b, ln_g, ln_b)


def _attn_kernel(bounded_ref, lq1_ref, lk1_ref, lq2_ref, lk2_ref, sg_ref, q_ref, k_ref, v_ref, *refs,
                 tq, lam_init, n_cast):
    cast_in, o_ref, cast_out = refs[:n_cast], refs[n_cast], refs[n_cast + 1:2 * n_cast + 1]
    acc1_ref, acc2_ref = refs[2 * n_cast + 1:]

    def cast_weight_slabs():
        for src_ref, dst_ref in zip(cast_in, cast_out):
            dst_ref[...] = src_ref[...].astype(BF16)

    i = pl.program_id(2)
    hq = tq // 2
    nt = (((1,), (1,)), ((), ()))
    accs = (acc1_ref, acc2_ref)

    def block(ref, j):
        return ref[pl.ds(pl.multiple_of(j * tq, tq), tq), :]

    def qk(qb, kb):
        return (lax.dot_general(qb[:, :HEAD_DIM], kb[:, :HEAD_DIM], nt, preferred_element_type=F32),
                lax.dot_general(qb[:, HEAD_DIM:], kb[:, HEAD_DIM:], nt, preferred_element_type=F32))

    def pv(p, vb):
        return jnp.dot(p.astype(BF16), vb, preferred_element_type=F32)

    def rowsum(p):
        return jnp.sum(p, axis=-1, keepdims=True)

    def visible(n_rows, n_cols, row0):
        rows = (lax.broadcasted_iota(jnp.int32, (n_rows, n_cols), 0) + row0) // CHUNK
        cols = lax.broadcasted_iota(jnp.int32, (n_rows, n_cols), 1) // CHUNK
        return cols <= rows

    def finish(l1, l2):
        lam = (jnp.exp(jnp.sum(lq1_ref[...] * lk1_ref[...], keepdims=True))
               - jnp.exp(jnp.sum(lq2_ref[...] * lk2_ref[...], keepdims=True)) + lam_init)
        o = acc1_ref[...] * (1.0 / l1) - lam * (acc2_ref[...] * (1.0 / l2))
        o_ref[...] = (_rms(o) * sg_ref[...] * (1.0 - lam_init)).astype(BF16)

    @pl.when(bounded_ref[0] != 0)
    def _():
        kd, vd = block(k_ref, i), block(v_ref, i)
        top = qk(q_ref[:hq, :], kd[:hq, :])
        bot = qk(q_ref[hq:, :], kd)
        vis_top, vis_bot = visible(hq, hq, 0), visible(hq, tq, hq)
        cast_weight_slabs()
        init = []
        for s_top, s_bot, acc_ref in zip(top, bot, accs):
            p_top = jnp.where(vis_top, jnp.exp2(s_top), 0.0)
            p_bot = jnp.where(vis_bot, jnp.exp2(s_bot), 0.0)
            acc_ref[:hq, :] = pv(p_top, vd[:hq, :])
            acc_ref[hq:, :] = pv(p_bot, vd)
            init.append(jnp.concatenate([rowsum(p_top), rowsum(p_bot)], axis=0))

        def body(j, ls, n_blocks=1):
            rows = pl.ds(pl.multiple_of(j * tq, tq), n_blocks * tq)
            out = []
            for s, acc_ref, l in zip(qk(q_ref[...], k_ref[rows, :]), accs, ls):
                p = jnp.exp2(s)
                acc_ref[...] += pv(p, v_ref[rows, :])
                out.append(l + rowsum(p))
            return tuple(out)

        ls = lax.fori_loop(0, lax.shift_right_logical(i, 1), lambda jj, ls: body(2 * jj, ls, 2), tuple(init))
        finish(*lax.fori_loop(i - lax.bitwise_and(i, 1), i, body, ls))

    @pl.when(bounded_ref[0] == 0)
    def _():
        q = q_ref[...]
        vis = visible(tq, tq, 0)
        cast_weight_slabs()

        def scores(j):
            return qk(q, block(k_ref, j))

        init = []
        for s, acc_ref in zip(scores(i), accs):
            s = jnp.where(vis, s, -jnp.inf)
            m = jnp.max(s, axis=-1, keepdims=True)
            p = jnp.exp2(s - m)
            acc_ref[...] = pv(p, block(v_ref, i))
            init += [m, rowsum(p)]

        def body(j, carry):
            out = []
            for s, acc_ref, m, l in zip(scores(j), accs, carry[0::2], carry[1::2]):
                m_new = jnp.maximum(m, jnp.max(s, axis=-1, keepdims=True))
                a = jnp.exp2(m - m_new)
                p = jnp.exp2(s - m_new)
                acc_ref[...] = a * acc_ref[...] + pv(p, block(v_ref, j))
                out += [m_new, a * l + rowsum(p)]
            return tuple(out)

        _, l1, _, l2 = lax.fori_loop(0, i, body, tuple(init))
        finish(l1, l2)


def _attention(bounded, lams, sg, q, k, v, weights, *, batch, seq, tq, lam_init):
    q3, k3, v3 = (a.reshape(batch, seq, a.shape[-1]) for a in (q, k, v))
    nq = seq // tq
    steps = batch * ATT_HEADS * nq
    blk = pl.BlockSpec((None, tq, V_DIM), lambda b, h, i: (b, i, h))
    whole = pl.BlockSpec((None, seq, V_DIM), lambda b, h, i: (b, 0, h))
    slabs = []
    for w in weights:
        group = 1
        while w.shape[0] % (steps // group) or (w.shape[0] // (steps // group)) % BF16_ROWS:
            group *= 2
        rows = w.shape[0] // (steps // group)
        slabs.append(pl.BlockSpec((rows, w.shape[1]),
                                  lambda b, h, i, g=group: (((b * ATT_HEADS + h) * nq + i) // g, 0)))
    outs = pl.pallas_call(
        functools.partial(_attn_kernel, tq=tq, lam_init=lam_init, n_cast=len(weights)),
        grid=(batch, ATT_HEADS, nq),
        in_specs=[pl.BlockSpec(memory_space=pltpu.SMEM)] + [_resident((1, HEAD_DIM))] * 4
                 + [_resident((1, V_DIM)), blk, whole, whole] + slabs,
        out_specs=[blk] + slabs,
        out_shape=[jax.ShapeDtypeStruct((batch, seq, ATT_WIDTH), BF16)]
                  + [jax.ShapeDtypeStruct(w.shape, BF16) for w in weights],
        scratch_shapes=[pltpu.VMEM((tq, V_DIM), F32), pltpu.VMEM((tq, V_DIM), F32)],
        compiler_params=_params("arbitrary", "arbitrary", "arbitrary"),
        name="diff_attn",
    )(bounded, *lams, sg, q3, k3, v3, *weights)
    return outs[0].reshape(batch * seq, ATT_WIDTH), outs[1:]


def _out_proj_kernel(x_ref, att_ref, uc_ref, w_ref, h_ref, *, att_w):
    h = x_ref[...] + jnp.dot(att_ref[...], w_ref[:att_w, :], preferred_element_type=F32)
    h_ref[...] = h + jnp.dot(uc_ref[...], w_ref[att_w:, :], preferred_element_type=F32)


def _out_proj(x2, att, uc, w_out, *, tm):
    T, D = x2.shape
    row = lambda w: pl.BlockSpec((tm, w), lambda i: (i, 0))
    return pl.pallas_call(
        functools.partial(_out_proj_kernel, att_w=att.shape[1]),
        grid=(T // tm,),
        in_specs=[row(D), row(att.shape[1]), row(uc.shape[1]), _resident(w_out.shape)],
        out_specs=row(D),
        out_shape=jax.ShapeDtypeStruct((T, D), F32),
        compiler_params=_params("arbitrary"),
        name="out_proj",
    )(x2, att, uc, w_out)


def _ffn_kernel(h_ref, g2_ref, wg_ref, wu_ref, wd_ref, o_ref, hn_ref):
    f = pl.program_id(1)

    @pl.when(f == 0)
    def _():
        h = h_ref[...]
        hn_ref[...] = (_rms(h) * g2_ref[...]).astype(BF16)
        o_ref[...] = h

    hn = hn_ref[...]
    a = jnp.dot(hn, wg_ref[...], preferred_element_type=F32)
    b = jnp.dot(hn, wu_ref[...], preferred_element_type=F32)
    g = (a * jax.nn.sigmoid(a) * b).astype(BF16)
    o_ref[...] += jnp.dot(g, wd_ref[...], preferred_element_type=F32)


def _ffn(h, g2, wg, wu, wd, *, tm, tf):
    T, D = h.shape
    F = wg.shape[1]
    return pl.pallas_call(
        _ffn_kernel,
        grid=(T // tm, F // tf),
        in_specs=[pl.BlockSpec((tm, D), lambda i, f: (i, 0)), _resident((1, D)),
                  pl.BlockSpec((D, tf), lambda i, f: (0, f)), pl.BlockSpec((D, tf), lambda i, f: (0, f)),
                  pl.BlockSpec((tf, D), lambda i, f: (f, 0))],
        out_specs=pl.BlockSpec((tm, D), lambda i, f: (i, 0)),
        out_shape=jax.ShapeDtypeStruct((T, D), F32),
        scratch_shapes=[pltpu.VMEM((tm, D), BF16)],
        compiler_params=_params("arbitrary", "arbitrary"),
        name="swiglu_ffn",
    )(h, g2, wg, wu, wd)


def _tile(n, want):
    t = want
    while n % t:
        t //= 2
    return t


def _rope_tables(seq):
    inv_freq = ROPE_THETA ** (-jnp.arange(0, HEAD_DIM, 2, dtype=F32) / HEAD_DIM)
    ang = jnp.arange(seq, dtype=F32)[:, None] * inv_freq[None, :]
    ang = jnp.concatenate([ang, ang], axis=-1)
    sign = jnp.where(jnp.arange(HEAD_DIM) < HEAD_DIM // 2, -1.0, 1.0).astype(F32)
    return jnp.cos(ang), jnp.sin(ang) * sign


def kernel(x, norm1_g, w_in, q_norm_g, k_norm_g, lambda_q1, lambda_k1, lambda_q2, lambda_k2, subln_g,
           conv_w, conv_b, conv_ln_g, conv_ln_b, w_out, norm2_g, w_gate, w_up, w_down):
    B, S, D = x.shape
    T = B * S
    F = w_gate.shape[-1]
    cos, sin = _rope_tables(S)
    row = lambda a: a.reshape(1, -1).astype(F32)
    h = x.reshape(T, D)
    for l in range(w_in.shape[0]):
        lam_init = 0.8 - 0.6 * math.exp(-0.3 * l)
        q, k, v, uc = _in_proj(h, row(norm1_g[l]), w_in[l].astype(BF16), row(q_norm_g[l]), row(k_norm_g[l]),
                               cos, sin, conv_w[l].reshape(CONV_WIDTH, -1).astype(F32), row(conv_b[l]),
                               row(conv_ln_g[l]), row(conv_ln_b[l]), seq=S, tm=_tile(S, 512))
        lams = [row(a[l]) for a in (lambda_q1, lambda_k1, lambda_q2, lambda_k2)]
        score_bound = 1.02 * math.sqrt(HEAD_DIM) * jnp.max(jnp.abs(q_norm_g[l])) * jnp.max(jnp.abs(k_norm_g[l]))
        bounded = (score_bound <= SAFE_SCORE).astype(jnp.int32).reshape(1)
        att, (wo, wg, wu, wd) = _attention(bounded, lams, row(subln_g[l]), q, k, v,
                                           [w_out[l], w_gate[l], w_up[l], w_down[l]],
                                           batch=B, seq=S, tq=_tile(S, 1024), lam_init=lam_init)
        h = _out_proj(h, att, uc, wo, tm=_tile(T, 512))
        h = _ffn(h, row(norm2_g[l]), wg, wu, wd, tm=_tile(T, 1024), tf=_tile(F, 512))
    return h.reshape(B, S, D)
```
